```python
import jax
import jax.numpy as jnp
from jax import lax
import numpy as np

D_MODEL = 1024
BATCH = 2
SEQ = 8192
DEPTH = 4
DEC_BATCH = 128
DEC_SEQ = 1
PAST_LEN = 2048
PAGE_SIZE = 128

N_MIXERS = 3
N_FOX = (DEPTH + 2) // 3
N_GDN = (DEPTH + 1) // 3
N_RWKV = DEPTH // 3
NORM_EPS = 1e-6

FOX_HEAD_DIM = 64
FOX_HEADS = D_MODEL // FOX_HEAD_DIM
FOX_WIDTH = FOX_HEADS * FOX_HEAD_DIM
Q_BLOCK = 128

GDN_HEAD_DIM = 128
GDN_QK_HEADS = D_MODEL // GDN_HEAD_DIM
GDN_V_HEADS = 2 * GDN_QK_HEADS
GDN_QK_WIDTH = GDN_QK_HEADS * GDN_HEAD_DIM
GDN_V_WIDTH = GDN_V_HEADS * GDN_HEAD_DIM
GDN_CONV_CH = 2 * GDN_QK_WIDTH + GDN_V_WIDTH
GDN_CONV_W = 4
GDN_CHUNK = 64

RWKV_HEAD_DIM = 64
RWKV_HEADS = D_MODEL // RWKV_HEAD_DIM
RWKV_DECAY_LORA = max(32, int(round(1.8 * D_MODEL ** 0.5 / 32)) * 32)
RWKV_AAA_LORA = max(32, int(round(1.8 * D_MODEL ** 0.5 / 32)) * 32)
RWKV_GATE_LORA = max(32, int(round(0.6 * D_MODEL ** 0.8 / 32)) * 32)
RWKV_GN_EPS = 64e-5

FFN_HIDDEN = -(-8 * D_MODEL // (3 * 256)) * 256

kernel_name = "hybrid_fox_gdn_rwkv7_decode_step"


def rms_norm(x, gain):
    xf = x.astype(jnp.float32)
    y = xf * lax.rsqrt(jnp.mean(xf * xf, axis=-1, keepdims=True) + NORM_EPS)
    return (y * gain.astype(jnp.float32)).astype(x.dtype)


def l2_normalize(x):
    xf = x.astype(jnp.float32)
    return xf * lax.rsqrt(jnp.sum(xf * xf, axis=-1, keepdims=True) + 1e-6)


def swiglu(u, w_in, w_out):
    gate, up = jnp.split(u @ w_in, 2, axis=-1)
    return (jax.nn.silu(gate) * up) @ w_out


def fox_project(u, w_in, b_f, q_gain, k_gain):
    b, t, _ = u.shape
    h = u @ w_in
    q, k, v, g, f = jnp.split(h, [FOX_WIDTH, 2 * FOX_WIDTH, 3 * FOX_WIDTH, 4 * FOX_WIDTH], axis=-1)
    shp = (b, t, FOX_HEADS, FOX_HEAD_DIM)
    q = rms_norm(q.reshape(shp), q_gain)
    k = rms_norm(k.reshape(shp), k_gain)
    log_f = jax.nn.log_sigmoid((f + b_f).astype(jnp.float32))
    return q, k, v.reshape(shp), jax.nn.sigmoid(g), log_f


def fox_output(o, gate, w_out):
    b, t = o.shape[:2]
    return (o.reshape(b, t, FOX_WIDTH) * gate) @ w_out


def fox_attend_prompt(q, k, v, log_f):
    b, t, h, dh = q.shape
    scale = dh ** -0.5
    cum = jnp.cumsum(log_f, axis=1)
    cum_t = jnp.swapaxes(cum, 1, 2)
    nb = t // Q_BLOCK
    qb = jnp.moveaxis(q.reshape(b, nb, Q_BLOCK, h, dh), 1, 0)
    cb = jnp.moveaxis(cum.reshape(b, nb, Q_BLOCK, h), 1, 0)
    k_pos = jnp.arange(t)

    def block(args):
        i, q_i, c_i = args
        s = jnp.einsum('bqhd,bkhd->bhqk', q_i, k).astype(jnp.float32) * scale
        s = s + jnp.swapaxes(c_i, 1, 2)[..., None] - cum_t[:, :, None, :]
        q_pos = i * Q_BLOCK + jnp.arange(Q_BLOCK)
        s = jnp.where(k_pos[None, :] <= q_pos[:, None], s, -jnp.inf)
        p = jax.nn.softmax(s, axis=-1)
        return jnp.einsum('bhqk,bkhd->bqhd', p.astype(v.dtype), v)

    o = lax.map(block, (jnp.arange(nb), qb, cb))
    return jnp.moveaxis(o, 0, 1).reshape(b, t, h, dh)


def fox_attend_sample(q, k, v, log_f, k_past, v_past, log_f_past):
    t, dh = q.shape[1], q.shape[3]
    p_len = k_past.shape[1]
    scale = dh ** -0.5
    cum = jnp.cumsum(jnp.concatenate([log_f_past.astype(jnp.float32), log_f], axis=1), axis=1)
    cum_t = jnp.swapaxes(cum, 1, 2)
    c_past, c_new = cum_t[..., :p_len], cum_t[..., p_len:]
    s_past = jnp.einsum('bqhd,bkhd->bhqk', q, k_past).astype(jnp.float32) * scale
    s_past = s_past + c_new[..., None] - c_past[:, :, None, :]
    s_new = jnp.einsum('bqhd,bkhd->bhqk', q, k).astype(jnp.float32) * scale
    s_new = s_new + c_new[..., None] - c_new[:, :, None, :]
    s_new = jnp.where(jnp.tril(jnp.ones((t, t), bool)), s_new, -jnp.inf)
    p = jax.nn.softmax(jnp.concatenate([s_past, s_new], axis=-1), axis=-1).astype(v.dtype)
    return (jnp.einsum('bhqk,bkhd->bqhd', p[..., :p_len], v_past)
            + jnp.einsum('bhqk,bkhd->bqhd', p[..., p_len:], v))


def gated_delta_rule(q, k, v, beta, g, s0):
    b, t, h, dk = q.shape
    dv = v.shape[-1]
    c = min(GDN_CHUNK, t)
    n = -(-t // c)
    pad = n * c - t

    def to_chunks(a):
        a = jnp.pad(a, [(0, 0), (0, pad)] + [(0, 0)] * (a.ndim - 2))
        return jnp.moveaxis(a.reshape((b, n, c) + a.shape[2:]), 3, 1)

    q, k, v, beta, g = (to_chunks(a) for a in (q, k, v, beta, g))
    gc = jnp.cumsum(g, axis=-1)
    incl = jnp.tril(jnp.ones((c, c), bool))
    strict = jnp.tril(jnp.ones((c, c), bool), -1)
    diff = gc[..., :, None] - gc[..., None, :]
    decay = jnp.where(incl, jnp.exp(jnp.where(incl, diff, 0.0)), 0.0)
    kb = k * beta[..., None]
    lower = jnp.where(strict, jnp.einsum('bhnid,bhnjd->bhnij', kb, k) * decay, 0.0)
    rhs = jnp.concatenate([v * beta[..., None], kb * jnp.exp(gc)[..., None]], axis=-1)
    sol = lax.linalg.triangular_solve(lower + jnp.eye(c, dtype=lower.dtype), rhs,
                                      left_side=True, lower=True, unit_diagonal=True)
    u, w = sol[..., :dv], sol[..., dv:]
    attn = jnp.where(incl, jnp.einsum('bhnid,bhnjd->bhnij', q, k) * decay, 0.0)
    q_dec = q * jnp.exp(gc)[..., None]
    k_dec = k * jnp.exp(gc[..., -1:] - gc)[..., None]
    last = jnp.exp(gc[..., -1])
    xs = tuple(jnp.moveaxis(a, 2, 0) for a in (u, w, attn, q_dec, k_dec, last))

    def step(s, inp):
        u_n, w_n, a_n, qd_n, kd_n, l_n = inp
        v_new = u_n - jnp.einsum('bhck,bhkv->bhcv', w_n, s)
        o_n = jnp.einsum('bhck,bhkv->bhcv', qd_n, s) + jnp.einsum('bhij,bhjv->bhiv', a_n, v_new)
        s = s * l_n[..., None, None] + jnp.einsum('bhck,bhcv->bhkv', kd_n, v_new)
        return s, o_n

    s, o = lax.scan(step, s0, xs)
    o = jnp.moveaxis(jnp.moveaxis(o, 0, 2), 1, 3).reshape(b, n * c, h, dv)[:, :t]
    return o, s


def gdn_mixer(u, conv_buf, s0, w_in, conv_w, a_log, dt_bias, out_gain, w_out):
    b, t, _ = u.shape
    f32 = jnp.float32
    h = u @ w_in
    qkv, z, a, beta_logit = jnp.split(
        h, [GDN_CONV_CH, GDN_CONV_CH + GDN_V_WIDTH, GDN_CONV_CH + GDN_V_WIDTH + GDN_V_HEADS], axis=-1)
    xc = jnp.concatenate([conv_buf.astype(qkv.dtype), qkv], axis=1)
    conv = xc[:, 0:t] * conv_w[0]
    for i in range(1, GDN_CONV_W):
        conv = conv + xc[:, i:i + t] * conv_w[i]
    conv = jax.nn.silu(conv)
    q, k, v = jnp.split(conv, [GDN_QK_WIDTH, 2 * GDN_QK_WIDTH], axis=-1)
    rep = GDN_V_HEADS // GDN_QK_HEADS
    qk_shape = (b, t, GDN_QK_HEADS, GDN_HEAD_DIM)
    q = jnp.repeat(l2_normalize(q.reshape(qk_shape)), rep, axis=2) * (GDN_HEAD_DIM ** -0.5)
    k = jnp.repeat(l2_normalize(k.reshape(qk_shape)), rep, axis=2)
    v = v.reshape(b, t, GDN_V_HEADS, GDN_HEAD_DIM).astype(f32)
    beta = jax.nn.sigmoid(beta_logit.astype(f32))
    g = -jnp.exp(a_log.astype(f32)) * jax.nn.softplus(a.astype(f32) + dt_bias.astype(f32))
    o, s = gated_delta_rule(q, k, v, beta, g, s0.astype(f32))
    o = rms_norm(o, out_gain) * jax.nn.silu(z.astype(f32).reshape(b, t, GDN_V_HEADS, GDN_HEAD_DIM))
    y = o.reshape(b, t, GDN_V_WIDTH).astype(u.dtype) @ w_out
    return y, xc[:, t:], s


def rwkv7_scan(r, w, k, v, a, bb, s0):
    def step(s, inp):
        r_t, w_t, k_t, v_t, a_t, b_t = inp
        sa = jnp.einsum('bhij,bhj->bhi', s, a_t)
        s = s * w_t[:, :, None, :] + sa[..., None] * b_t[:, :, None, :] + v_t[..., None] * k_t[:, :, None, :]
        return s, jnp.einsum('bhij,bhj->bhi', s, r_t)

    xs = tuple(jnp.moveaxis(t_, 1, 0) for t_ in (r, w, k, v, a, bb))
    s, y = lax.scan(step, s0, xs)
    return jnp.moveaxis(y, 0, 1), s


def rwkv7_mixer(u, shift, s0, mu, w0, w1, w2, a0, a1, a2, g1, g2, k_k, k_a, r_k,
                w_r, w_k, w_v, w_o, ln_w, ln_b):
    b, t, c = u.shape
    f32 = jnp.float32
    hd = (b, t, RWKV_HEADS, RWKV_HEAD_DIM)
    u_prev = jnp.concatenate([shift[:, None, :].astype(u.dtype), u[:, :-1]], axis=1)
    du = u_prev - u
    xr, xw, xk, xv, xa, xg = (u + du * mu[i] for i in range(6))
    r = (xr @ w_r).astype(f32).reshape(hd)
    k = (xk @ w_k).astype(f32)
    v = (xv @ w_v).astype(f32).reshape(hd)
    w_raw = -jax.nn.softplus(-(w0 + jnp.tanh(xw @ w1) @ w2).astype(f32)) - 0.5
    decay = jnp.exp(-jnp.exp(w_raw)).reshape(hd)
    a = jax.nn.sigmoid((a0 + (xa @ a1) @ a2).astype(f32))
    gate = jax.nn.sigmoid(xg @ g1) @ g2
    kk = l2_normalize((k * k_k).reshape(hd))
    k = (k * (1.0 + (a - 1.0) * k_a)).reshape(hd)
    a = a.reshape(hd)
    y, s = rwkv7_scan(r, decay, k, v, -kk, kk * a, s0.astype(f32))
    mean = jnp.mean(y, axis=-1, keepdims=True)
    var = jnp.mean(jnp.square(y - mean), axis=-1, keepdims=True)
    y = ((y - mean) * lax.rsqrt(var + RWKV_GN_EPS)).reshape(b, t, c) * ln_w + ln_b
    y = y + (jnp.sum(r * k * r_k, axis=-1, keepdims=True) * v).reshape(b, t, c)
    out = (y * gate).astype(u.dtype) @ w_o
    return out, u[:, -1], s


def setup_inputs(seed: int = 0) -> dict:
    key = jax.random.key(seed)
    keys = iter(jax.random.split(key, 64))
    f32 = jnp.float32

    def normal(shape, scale=1.0):
        return scale * jax.random.normal(next(keys), shape, f32)

    def uniform(shape, lo, hi):
        return jax.random.uniform(next(keys), shape, f32, lo, hi)

    n_pages = PAST_LEN // PAGE_SIZE
    n_used = DEC_BATCH * n_pages
    n_phys = n_used + max(1, n_used // 4)
    H, Dh, D = FOX_HEADS, FOX_HEAD_DIM, D_MODEL
    NR, NH = RWKV_HEAD_DIM, RWKV_HEADS

    x_prompt = normal((BATCH, SEQ, D))
    x_sample = normal((DEC_BATCH, DEC_SEQ, D))
    cache_k = normal((N_FOX, n_phys, PAGE_SIZE, H, Dh))
    cache_v = normal((N_FOX, n_phys, PAGE_SIZE, H, Dh))
    cache_logf = jax.nn.log_sigmoid(normal((N_FOX, n_phys, PAGE_SIZE, H)) + 3.0)
    page_table = jax.random.permutation(next(keys), n_phys)[:n_used].reshape(DEC_BATCH, n_pages).astype(jnp.int32)
    state_gdn_s = normal((N_GDN, DEC_BATCH, GDN_V_HEADS, GDN_HEAD_DIM, GDN_HEAD_DIM), 0.3)
    state_gdn_conv = normal((N_GDN, DEC_BATCH, GDN_CONV_W - 1, GDN_CONV_CH))
    state_rwkv_shift = normal((N_RWKV, DEC_BATCH, D))
    state_rwkv_wkv = normal((N_RWKV, DEC_BATCH, NH, NR, NR), 0.5)

    norm_mix = 1.0 + normal((DEPTH, D), 0.01)
    norm_ffn = 1.0 + normal((DEPTH, D), 0.01)
    norm_final = 1.0 + normal((D,), 0.01)

    fox_w_in = normal((N_FOX, D, 4 * FOX_WIDTH + H), D ** -0.5)
    fox_b_f = uniform((N_FOX, H), 1.0, 4.0)
    fox_q_norm = 1.0 + normal((N_FOX, Dh), 0.01)
    fox_k_norm = 1.0 + normal((N_FOX, Dh), 0.01)
    fox_w_out = normal((N_FOX, FOX_WIDTH, D), FOX_WIDTH ** -0.5)

    gdn_w_in = normal((N_GDN, D, GDN_CONV_CH + GDN_V_WIDTH + 2 * GDN_V_HEADS), D ** -0.5)
    gdn_conv_w = normal((N_GDN, GDN_CONV_W, GDN_CONV_CH), GDN_CONV_W ** -0.5)
    gdn_a_log = jnp.log(uniform((N_GDN, GDN_V_HEADS), 1.0, 16.0))
    dt = jnp.exp(uniform((N_GDN, GDN_V_HEADS), float(np.log(1e-3)), float(np.log(1e-1))))
    gdn_dt_bias = dt + jnp.log(-jnp.expm1(-dt))
    gdn_out_norm = 1.0 + normal((N_GDN, GDN_HEAD_DIM), 0.01)
    gdn_w_out = normal((N_GDN, GDN_V_WIDTH, D), GDN_V_WIDTH ** -0.5)

    rwkv_mu = uniform((N_RWKV, 6, D), 0.0, 1.0)
    rwkv_w0 = uniform((N_RWKV, D), -6.0, -1.0)
    rwkv_w1 = normal((N_RWKV, D, RWKV_DECAY_LORA), 0.1 * D ** -0.5)
    rwkv_w2 = normal((N_RWKV, RWKV_DECAY_LORA, D), 0.1 * RWKV_DECAY_LORA ** -0.5)
    rwkv_a0 = normal((N_RWKV, D), 0.1)
    rwkv_a1 = normal((N_RWKV, D, RWKV_AAA_LORA), 0.1 * D ** -0.5)
    rwkv_a2 = normal((N_RWKV, RWKV_AAA_LORA, D), 0.1 * RWKV_AAA_LORA ** -0.5)
    rwkv_g1 = normal((N_RWKV, D, RWKV_GATE_LORA), D ** -0.5)
    rwkv_g2 = normal((N_RWKV, RWKV_GATE_LORA, D), RWKV_GATE_LORA ** -0.5)
    rwkv_k_k = 0.85 + normal((N_RWKV, D), 0.05)
    rwkv_k_a = 1.0 + normal((N_RWKV, D), 0.05)
    rwkv_r_k = normal((N_RWKV, NH, NR), 0.1)
    rwkv_w_r = normal((N_RWKV, D, D), D ** -0.5)
    rwkv_w_k = normal((N_RWKV, D, D), D ** -0.5)
    rwkv_w_v = normal((N_RWKV, D, D), D ** -0.5)
    rwkv_w_o = normal((N_RWKV, D, D), D ** -0.5)
    rwkv_ln_w = 1.0 + normal((N_RWKV, D), 0.01)
    rwkv_ln_b = normal((N_RWKV, D), 0.01)

    ffn_w_in = normal((DEPTH, D, 2 * FFN_HIDDEN), D ** -0.5)
    ffn_w_out = normal((DEPTH, FFN_HIDDEN, D), FFN_HIDDEN ** -0.5)

    return {"x_prompt": x_prompt, "x_sample": x_sample, "cache_k": cache_k, "cache_v": cache_v,
            "cache_logf": cache_logf, "page_table": page_table, "state_gdn_s": state_gdn_s,
            "state_gdn_conv": state_gdn_conv, "state_rwkv_shift": state_rwkv_shift,
            "state_rwkv_wkv": state_rwkv_wkv, "norm_mix": norm_mix, "norm_ffn": norm_ffn,
            "norm_final": norm_final, "fox_w_in": fox_w_in, "fox_b_f": fox_b_f,
            "fox_q_norm": fox_q_norm, "fox_k_norm": fox_k_norm, "fox_w_out": fox_w_out,
            "gdn_w_in": gdn_w_in, "gdn_conv_w": gdn_conv_w, "gdn_a_log": gdn_a_log,
            "gdn_dt_bias": gdn_dt_bias, "gdn_out_norm": gdn_out_norm, "gdn_w_out": gdn_w_out,
            "rwkv_mu": rwkv_mu, "rwkv_w0": rwkv_w0, "rwkv_w1": rwkv_w1, "rwkv_w2": rwkv_w2,
            "rwkv_a0": rwkv_a0, "rwkv_a1": rwkv_a1, "rwkv_a2": rwkv_a2, "rwkv_g1": rwkv_g1,
            "rwkv_g2": rwkv_g2, "rwkv_k_k": rwkv_k_k, "rwkv_k_a": rwkv_k_a, "rwkv_r_k": rwkv_r_k,
            "rwkv_w_r": rwkv_w_r, "rwkv_w_k": rwkv_w_k, "rwkv_w_v": rwkv_w_v, "rwkv_w_o": rwkv_w_o,
            "rwkv_ln_w": rwkv_ln_w, "rwkv_ln_b": rwkv_ln_b, "ffn_w_in": ffn_w_in, "ffn_w_out": ffn_w_out}


def reference(x_prompt, x_sample, cache_k, cache_v, cache_logf, page_table, state_gdn_s,
              state_gdn_conv, state_rwkv_shift, state_rwkv_wkv, norm_mix, norm_ffn, norm_final,
              fox_w_in, fox_b_f, fox_q_norm, fox_k_norm, fox_w_out, gdn_w_in, gdn_conv_w,
              gdn_a_log, gdn_dt_bias, gdn_out_norm, gdn_w_out, rwkv_mu, rwkv_w0, rwkv_w1, rwkv_w2,
              rwkv_a0, rwkv_a1, rwkv_a2, rwkv_g1, rwkv_g2, rwkv_k_k, rwkv_k_a, rwkv_r_k, rwkv_w_r,
              rwkv_w_k, rwkv_w_v, rwkv_w_o, rwkv_ln_w, rwkv_ln_b, ffn_w_in, ffn_w_out):
    bp = x_prompt.shape[0]
    db = x_sample.shape[0]
    past_len = page_table.shape[1] * PAGE_SIZE
    hp, hs = x_prompt, x_sample
    fk_p, fv_p, ff_p, fk_s, fv_s, ff_s = [], [], [], [], [], []
    gs_p, gc_p, gs_s, gc_s = [], [], [], []
    rsh_p, rw_p, rsh_s, rw_s = [], [], [], []
    for layer in range(DEPTH):
        kind, j = layer % N_MIXERS, layer // N_MIXERS
        up = rms_norm(hp, norm_mix[layer])
        us = rms_norm(hs, norm_mix[layer])
        if kind == 0:
            fw = (fox_w_in[j], fox_b_f[j], fox_q_norm[j], fox_k_norm[j])
            q, k, v, g, lf = fox_project(up, *fw)
            mp = fox_output(fox_attend_prompt(q, k, v, lf), g, fox_w_out[j])
            fk_p.append(k.astype(cache_k.dtype)); fv_p.append(v.astype(cache_v.dtype)); ff_p.append(lf.astype(cache_logf.dtype))
            q, k, v, g, lf = fox_project(us, *fw)
            k_past = cache_k[j, page_table].reshape(db, past_len, FOX_HEADS, FOX_HEAD_DIM)
            v_past = cache_v[j, page_table].reshape(db, past_len, FOX_HEADS, FOX_HEAD_DIM)
            lf_past = cache_logf[j, page_table].reshape(db, past_len, FOX_HEADS)
            ms = fox_output(fox_attend_sample(q, k, v, lf, k_past, v_past, lf_past), g, fox_w_out[j])
            fk_s.append(k.astype(cache_k.dtype)); fv_s.append(v.astype(cache_v.dtype)); ff_s.append(lf.astype(cache_logf.dtype))
        elif kind == 1:
            gw = (gdn_w_in[j], gdn_conv_w[j], gdn_a_log[j], gdn_dt_bias[j], gdn_out_norm[j], gdn_w_out[j])
            conv0 = jnp.zeros((bp, GDN_CONV_W - 1, GDN_CONV_CH), up.dtype)
            s0 = jnp.zeros((bp, GDN_V_HEADS, GDN_HEAD_DIM, GDN_HEAD_DIM), jnp.float32)
            mp, cb, s = gdn_mixer(up, conv0, s0, *gw)
            gs_p.append(s.astype(state_gdn_s.dtype)); gc_p.append(cb.astype(state_gdn_conv.dtype))
            ms, cb, s = gdn_mixer(us, state_gdn_conv[j], state_gdn_s[j], *gw)
            gs_s.append(s.astype(state_gdn_s.dtype)); gc_s.append(cb.astype(state_gdn_conv.dtype))
        else:
            rw = (rwkv_mu[j], rwkv_w0[j], rwkv_w1[j], rwkv_w2[j], rwkv_a0[j], rwkv_a1[j], rwkv_a2[j],
                  rwkv_g1[j], rwkv_g2[j], rwkv_k_k[j], rwkv_k_a[j], rwkv_r_k[j], rwkv_w_r[j],
                  rwkv_w_k[j], rwkv_w_v[j], rwkv_w_o[j], rwkv_ln_w[j], rwkv_ln_b[j])
            sh0 = jnp.zeros((bp, D_MODEL), up.dtype)
            s0 = jnp.zeros((bp, RWKV_HEADS, RWKV_HEAD_DIM, RWKV_HEAD_DIM), jnp.float32)
            mp, sh, s = rwkv7_mixer(up, sh0, s0, *rw)
            rsh_p.append(sh.astype(state_rwkv_shift.dtype)); rw_p.append(s.astype(state_rwkv_wkv.dtype))
            ms, sh, s = rwkv7_mixer(us, state_rwkv_shift[j], state_rwkv_wkv[j], *rw)
            rsh_s.append(sh.astype(state_rwkv_shift.dtype)); rw_s.append(s.astype(state_rwkv_wkv.dtype))
        hp = hp + mp
        hs = hs + ms
        hp = hp + swiglu(rms_norm(hp, norm_ffn[layer]), ffn_w_in[layer], ffn_w_out[layer])
        hs = hs + swiglu(rms_norm(hs, norm_ffn[layer]), ffn_w_in[layer], ffn_w_out[layer])
    y_prompt = rms_norm(hp, norm_final)
    y_sample = rms_norm(hs, norm_final)
    return (y_prompt, y_sample,
            jnp.stack(fk_p), jnp.stack(fv_p), jnp.stack(ff_p),
            jnp.stack(fk_s), jnp.stack(fv_s), jnp.stack(ff_s),
            jnp.stack(gs_p), jnp.stack(gc_p), jnp.stack(gs_s), jnp.stack(gc_s),
            jnp.stack(rsh_p), jnp.stack(rw_p), jnp.stack(rsh_s), jnp.stack(rw_s))
```

```python
import functools

import jax
import jax.numpy as jnp
from jax import lax
from jax.experimental import pallas as pl
from jax.experimental.pallas import tpu as pltpu

F32 = jnp.float32
BF16 = jnp.bfloat16
HIGHEST = lax.Precision.HIGHEST

NORM_EPS = 1e-6
RWKV_GN_EPS = 64e-5
LANES = 128
VMEM_LIMIT_BYTES = 48 * 1024 * 1024
NEG_BIG = -1e30

FOX_HEAD_DIM = 64
GDN_HEAD_DIM = 128
GDN_CONV_W = 4
RWKV_HEAD_DIM = 64
CHUNK = 64
FOX_Q_TILE = 512


def _params(*semantics):
    return pltpu.CompilerParams(dimension_semantics=semantics, vmem_limit_bytes=VMEM_LIMIT_BYTES)


def _sigmoid(x):
    return 1.0 / (1.0 + jnp.exp(-x))


def _rms(x, gain):
    return x * lax.rsqrt(jnp.mean(x * x, axis=-1, keepdims=True) + NORM_EPS) * gain


def _dot(a, b):
    return jnp.dot(a, b, preferred_element_type=F32)


def _dot_nt(a, b):
    return lax.dot_general(a, b, (((1,), (1,)), ((), ())), preferred_element_type=F32)


def _dot_tn(a, b):
    return lax.dot_general(a, b, (((0,), (0,)), ((), ())), preferred_element_type=F32)


def _dot_hi(a, b):
    return jnp.dot(a, b, preferred_element_type=F32, precision=HIGHEST)


def _dot_nt_hi(a, b):
    return lax.dot_general(a, b, (((1,), (1,)), ((), ())), preferred_element_type=F32, precision=HIGHEST)


def _dot_tn_hi(a, b):
    return lax.dot_general(a, b, (((0,), (0,)), ((), ())), preferred_element_type=F32, precision=HIGHEST)


def _bf(x):
    return x.astype(BF16)


def _pad_cols(w, n):
    return jnp.pad(w, ((0, 0), (0, n - w.shape[1])))


def _pad_rows(w, n):
    return jnp.pad(w, ((0, n - w.shape[0]), (0, 0)))


def _col_tile(n, cap=1536):
    best = LANES
    for t in range(LANES, min(n, cap) + 1, LANES):
        if n % t == 0:
            best = t
    return best


def _mm_kernel(*refs, norm, residual):
    it = iter(refs)
    x_ref = next(it)
    g_ref = next(it) if norm else None
    w_ref = next(it)
    r_ref = next(it) if residual else None
    o_ref = next(it)
    xn_ref = next(it) if norm else None
    if norm:
        @pl.when(pl.program_id(1) == 0)
        def _():
            xn_ref[...] = _bf(_rms(x_ref[...], g_ref[...]))
        a = xn_ref[...]
    else:
        a = _bf(x_ref[...])
    acc = _dot(a, w_ref[...])
    if residual:
        acc = acc + r_ref[...]
    o_ref[...] = acc.astype(o_ref.dtype)


def mm(x, w, *, gain=None, res=None, out_dtype=F32):
    m, k = x.shape
    n = w.shape[1]
    tm = min(m, 512)
    tn = _col_tile(n)
    norm, residual = gain is not None, res is not None
    in_specs = [pl.BlockSpec((tm, k), lambda i, j: (i, 0))]
    args = [x]
    if norm:
        in_specs.append(pl.BlockSpec((1, k), lambda i, j: (0, 0)))
        args.append(gain.reshape(1, k))
    in_specs.append(pl.BlockSpec((k, tn), lambda i, j: (0, j)))
    args.append(w)
    if residual:
        in_specs.append(pl.BlockSpec((tm, tn), lambda i, j: (i, j)))
        args.append(res)
    return pl.pallas_call(
        functools.partial(_mm_kernel, norm=norm, residual=residual),
        grid=(m // tm, n // tn),
        in_specs=in_specs,
        out_specs=pl.BlockSpec((tm, tn), lambda i, j: (i, j)),
        out_shape=jax.ShapeDtypeStruct((m, n), out_dtype),
        scratch_shapes=[pltpu.VMEM((tm, k), BF16)] if norm else [],
        compiler_params=_params("parallel", "arbitrary"),
    )(*args)


def _ffn_kernel(*refs, final):
    if final:
        x_ref, g_ref, wg_ref, wu_ref, wo_ref, fg_ref, o_ref, xn_ref, acc_ref = refs
    else:
        x_ref, g_ref, wg_ref, wu_ref, wo_ref, o_ref, xn_ref, acc_ref = refs
    k = pl.program_id(1)

    @pl.when(k == 0)
    def _():
        xn_ref[...] = _bf(_rms(x_ref[...], g_ref[...]))
        acc_ref[...] = jnp.zeros_like(acc_ref)

    xn = xn_ref[...]
    gate = _dot(xn, wg_ref[...])
    up = _dot(xn, wu_ref[...])
    act = _bf(gate * _sigmoid(gate) * up)
    acc_ref[...] += _dot(act, wo_ref[...])

    @pl.when(k == pl.num_programs(1) - 1)
    def _():
        out = x_ref[...] + acc_ref[...]
        if final:
            out = _rms(out, fg_ref[...])
        o_ref[...] = out


def ffn(x, gain, w_in, w_out, final_gain=None):
    m, d = x.shape
    hidden = w_out.shape[0]
    th = 256
    nk = hidden // th
    tm = min(m, 512)
    final = final_gain is not None
    in_specs = [
        pl.BlockSpec((tm, d), lambda i, k: (i, 0)),
        pl.BlockSpec((1, d), lambda i, k: (0, 0)),
        pl.BlockSpec((d, th), lambda i, k: (0, k)),
        pl.BlockSpec((d, th), lambda i, k: (0, k + nk)),
        pl.BlockSpec((th, d), lambda i, k: (k, 0)),
    ]
    args = [x, gain.reshape(1, d), w_in, w_in, w_out]
    if final:
        in_specs.append(pl.BlockSpec((1, d), lambda i, k: (0, 0)))
        args.append(final_gain.reshape(1, d))
    return pl.pallas_call(
        functools.partial(_ffn_kernel, final=final),
        grid=(m // tm, nk),
        in_specs=in_specs,
        out_specs=pl.BlockSpec((tm, d), lambda i, k: (i, 0)),
        out_shape=jax.ShapeDtypeStruct((m, d), F32),
        scratch_shapes=[pltpu.VMEM((tm, d), BF16), pltpu.VMEM((tm, d), F32)],
        compiler_params=_params("parallel", "arbitrary"),
    )(*args)


def _fox_flash_kernel(q_ref, k_ref, v_ref, cc_ref, cr_ref, g_ref, o_ref, m_ref, l_ref, acc_ref, *, tq):
    i = pl.program_id(2)
    dh = FOX_HEAD_DIM
    m_ref[...] = jnp.full_like(m_ref, NEG_BIG)
    l_ref[...] = jnp.zeros_like(l_ref)
    acc_ref[...] = jnp.zeros_like(acc_ref)
    row = lax.broadcasted_iota(jnp.int32, (tq, tq), 0)
    col = lax.broadcasted_iota(jnp.int32, (tq, tq), 1)

    def block(j, masked):
        off = pl.multiple_of(j * tq, tq)
        for hh in range(2):
            sl = slice(hh * dh, (hh + 1) * dh)
            q = q_ref[:, sl]
            k = k_ref[pl.ds(off, tq), sl]
            v = v_ref[pl.ds(off, tq), sl]
            s = _dot_nt(q, k) + (cc_ref[:, hh:hh + 1] - cr_ref[j, pl.ds(hh, 1), :])
            if masked:
                s = jnp.where(row >= col, s, NEG_BIG)
            m_prev = m_ref[hh]
            m_new = jnp.maximum(m_prev, jnp.max(s, axis=-1, keepdims=True))
            alpha = jnp.exp(m_prev - m_new)
            p = jnp.exp(s - m_new)
            l_ref[hh] = alpha * l_ref[hh] + jnp.sum(p, axis=-1, keepdims=True)
            acc_ref[hh] = alpha * acc_ref[hh] + _dot(_bf(p), v)
            m_ref[hh] = m_new

    def body(j, carry):
        block(j, False)
        return carry

    lax.fori_loop(0, i, body, 0)
    block(i, True)
    for hh in range(2):
        sl = slice(hh * dh, (hh + 1) * dh)
        o_ref[:, sl] = _bf(acc_ref[hh] / l_ref[hh] * g_ref[:, sl])


def fox_flash(qb, kb, vb, cc, cr, gate, *, batch, seq):
    m, width = qb.shape
    n_pairs = width // LANES
    nq, tq = cr.shape[2], cr.shape[4]
    return pl.pallas_call(
        functools.partial(_fox_flash_kernel, tq=tq),
        grid=(batch, n_pairs, nq),
        in_specs=[
            pl.BlockSpec((tq, LANES), lambda b, h, i: (b * nq + i, h)),
            pl.BlockSpec((seq, LANES), lambda b, h, i: (b, h)),
            pl.BlockSpec((seq, LANES), lambda b, h, i: (b, h)),
            pl.BlockSpec((None, tq, 2), lambda b, h, i: (h, b * nq + i, 0)),
            pl.BlockSpec((None, None, nq, 2, tq), lambda b, h, i: (b, h, 0, 0, 0)),
            pl.BlockSpec((tq, LANES), lambda b, h, i: (b * nq + i, h)),
        ],
        out_specs=pl.BlockSpec((tq, LANES), lambda b, h, i: (b * nq + i, h)),
        out_shape=jax.ShapeDtypeStruct((m, width), BF16),
        scratch_shapes=[pltpu.VMEM((2, tq, 1), F32), pltpu.VMEM((2, tq, 1), F32),
                        pltpu.VMEM((2, tq, FOX_HEAD_DIM), F32)],
        compiler_params=_params("parallel", "parallel", "arbitrary"),
    )(qb, kb, vb, cc, cr, gate)


def _fox_decode_kernel(pt_ref, q_ref, kn_ref, vn_ref, bias_ref, kc_ref, vc_ref, o_ref, m_ref, l_ref, acc_ref):
    p = pl.program_id(1)
    n_heads, dh = q_ref.shape
    page = kc_ref.shape[0]
    rows = page * n_heads

    @pl.when(p == 0)
    def _():
        m_ref[...] = jnp.full_like(m_ref, NEG_BIG)
        l_ref[...] = jnp.zeros_like(l_ref)
        acc_ref[...] = jnp.zeros_like(acc_ref)

    q = q_ref[...]
    kf = _bf(kc_ref[...].reshape(rows, dh))
    vf = _bf(vc_ref[...].reshape(rows, dh))
    s = _dot_nt(_bf(q), kf) + bias_ref[...]
    head = lax.broadcasted_iota(jnp.int32, (n_heads, rows), 0)
    lane = lax.broadcasted_iota(jnp.int32, (n_heads, rows), 1)
    own = jnp.bitwise_and(lane, n_heads - 1) == head
    s = jnp.where(own, s, NEG_BIG)
    m_prev = m_ref[...]
    m_new = jnp.maximum(m_prev, jnp.max(s, axis=-1, keepdims=True))
    alpha = jnp.exp(m_prev - m_new)
    pr = jnp.where(own, jnp.exp(s - m_new), 0.0)
    l_ref[...] = alpha * l_ref[...] + jnp.sum(pr, axis=-1, keepdims=True)
    acc_ref[...] = alpha * acc_ref[...] + _dot(_bf(pr), vf)
    m_ref[...] = m_new

    @pl.when(p == pl.num_programs(1) - 1)
    def _():
        s_new = jnp.sum(_bf(q).astype(F32) * _bf(kn_ref[...]).astype(F32), axis=-1, keepdims=True)
        m_fin = jnp.maximum(m_ref[...], s_new)
        a = jnp.exp(m_ref[...] - m_fin)
        p_new = jnp.exp(s_new - m_fin)
        num = a * acc_ref[...] + _bf(p_new).astype(F32) * _bf(vn_ref[...]).astype(F32)
        o_ref[...] = num / (a * l_ref[...] + p_new)


def fox_decode(layer_idx, page_table, q, k_new, v_new, bias, cache_k, cache_v):
    db, n_heads, dh = q.shape
    assert n_heads & (n_heads - 1) == 0
    n_pages = page_table.shape[1]
    page = cache_k.shape[2]
    vec = pl.BlockSpec((None, n_heads, dh), lambda b, p, pt: (b, 0, 0))
    cache = pl.BlockSpec((None, None, page, n_heads, dh), lambda b, p, pt: (layer_idx, pt[b, p], 0, 0, 0))
    return pl.pallas_call(
        _fox_decode_kernel,
        grid_spec=pltpu.PrefetchScalarGridSpec(
            num_scalar_prefetch=1,
            grid=(db, n_pages),
            in_specs=[vec, vec, vec,
                      pl.BlockSpec((None, None, 1, page * n_heads), lambda b, p, pt: (b, p, 0, 0)),
                      cache, cache],
            out_specs=vec,
            scratch_shapes=[pltpu.VMEM((n_heads, 1), F32), pltpu.VMEM((n_heads, 1), F32),
                            pltpu.VMEM((n_heads, dh), F32)],
        ),
        out_shape=jax.ShapeDtypeStruct((db, n_heads, dh), F32),
        compiler_params=_params("parallel", "arbitrary"),
    )(page_table, q, k_new, v_new, bias, cache_k, cache_v)


def _tri_masks(c):
    row = lax.broadcasted_iota(jnp.int32, (c, c), 0)
    col = lax.broadcasted_iota(jnp.int32, (c, c), 1)
    return row >= col, row > col, row == col


def _neumann_inverse(a, eye, c):
    p = eye + a
    ak = a
    steps = max(c.bit_length() - 2, 0)
    for _ in range(steps):
        ak = _dot_hi(ak, ak)
        p = p + _dot_hi(p, ak)
    return p


def _gdn_chunk_kernel(q_ref, k_ref, v_ref, z_ref, bcol_ref, gcol_ref, grow_ref, gain_ref, o_ref, s_out_ref, s_ref, *, c, hb):
    ch = pl.program_id(2)
    dk = GDN_HEAD_DIM

    @pl.when(ch == 0)
    def _():
        s_ref[...] = jnp.zeros_like(s_ref)

    incl, strict, diag = _tri_masks(c)
    eye = jnp.where(diag, 1.0, 0.0).astype(F32)
    tri = jnp.where(incl, 1.0, 0.0).astype(F32)
    gc_col = _dot_hi(tri, gcol_ref[...])
    gc_row = _dot_nt_hi(grow_ref[...], tri)
    for hh in range(hb):
        qk = slice((hh // 2) * dk, (hh // 2 + 1) * dk)
        vs = slice(hh * dk, (hh + 1) * dk)
        q = q_ref[:, qk]
        k = k_ref[:, qk]
        v = v_ref[:, vs]
        beta = bcol_ref[:, hh:hh + 1]
        gcc = gc_col[:, hh:hh + 1]
        gcr = gc_row[hh:hh + 1, :]
        g_last = gcc[c - 1:c, :]
        decay = jnp.where(incl, jnp.exp(jnp.where(incl, gcc - gcr, 0.0)), 0.0)
        kb = k * beta
        kbf = _bf(k)
        lower = jnp.where(strict, _dot_nt(_bf(kb), kbf) * decay, 0.0)
        t_inv = _neumann_inverse(-lower, eye, c)
        e_gc = jnp.exp(gcc)
        rhs = jnp.concatenate([v * beta, kb * e_gc], axis=-1)
        sol = _dot_hi(t_inv, rhs)
        u = sol[:, :dk]
        w = sol[:, dk:]
        attn = jnp.where(incl, _dot_nt(_bf(q), kbf) * decay, 0.0)
        q_dec = q * e_gc
        k_dec = k * jnp.exp(g_last - gcc)
        s = s_ref[hh]
        sb = _bf(s)
        v_new = u - _dot(_bf(w), sb)
        vnb = _bf(v_new)
        o = _dot(_bf(q_dec), sb) + _dot(_bf(attn), vnb)
        s_ref[hh] = s * jnp.exp(g_last) + _dot_tn(_bf(k_dec), vnb)
        z = z_ref[:, vs]
        o_ref[:, vs] = _bf(_rms(o, gain_ref[...]) * (z * _sigmoid(z)))

    @pl.when(ch == pl.num_programs(2) - 1)
    def _():
        s_out_ref[...] = s_ref[...]


def gdn_chunk_scan(q, k, v, z, beta, g, gain, *, batch, seq):
    m = q.shape[0]
    n_heads = v.shape[1] // GDN_HEAD_DIM
    c = min(CHUNK, seq)
    hb = min(8, n_heads)
    n_groups = n_heads // hb
    n = seq // c
    bcol = beta.reshape(m, n_groups, hb).transpose(1, 0, 2)
    gcol = g.reshape(m, n_groups, hb).transpose(1, 0, 2)
    grow = g.reshape(batch * n, c, n_groups, hb).transpose(0, 2, 3, 1)
    qk_w = (hb // 2) * GDN_HEAD_DIM
    v_w = hb * GDN_HEAD_DIM
    row_blk = lambda b, h, t: (b * n + t, h)
    col_spec = pl.BlockSpec((None, c, hb), lambda b, h, t: (h, b * n + t, 0))
    return pl.pallas_call(
        functools.partial(_gdn_chunk_kernel, c=c, hb=hb),
        grid=(batch, n_groups, n),
        in_specs=[
            pl.BlockSpec((c, qk_w), row_blk),
            pl.BlockSpec((c, qk_w), row_blk),
            pl.BlockSpec((c, v_w), row_blk),
            pl.BlockSpec((c, v_w), row_blk),
            col_spec, col_spec,
            pl.BlockSpec((None, None, hb, c), lambda b, h, t: (b * n + t, h, 0, 0)),
            pl.BlockSpec((1, GDN_HEAD_DIM), lambda b, h, t: (0, 0)),
        ],
        out_specs=[
            pl.BlockSpec((c, v_w), row_blk),
            pl.BlockSpec((None, hb, GDN_HEAD_DIM, GDN_HEAD_DIM), lambda b, h, t: (b, h, 0, 0)),
        ],
        out_shape=[jax.ShapeDtypeStruct((m, n_heads * GDN_HEAD_DIM), BF16),
                   jax.ShapeDtypeStruct((batch, n_heads, GDN_HEAD_DIM, GDN_HEAD_DIM), F32)],
        scratch_shapes=[pltpu.VMEM((hb, GDN_HEAD_DIM, GDN_HEAD_DIM), F32)],
        compiler_params=_params("parallel", "parallel", "arbitrary"),
    )(q, k, v, z, bcol, gcol, grow, gain.reshape(1, GDN_HEAD_DIM))


def _gdn_step_kernel(q_ref, k_ref, v_ref, z_ref, beta_ref, g_ref, gain_ref, s0_ref, o_ref, s_ref):
    dk = GDN_HEAD_DIM
    n_heads = s0_ref.shape[0]
    row = lax.broadcasted_iota(jnp.int32, (8, dk), 0)
    for h in range(n_heads):
        qk = slice((h // 2) * dk, (h // 2 + 1) * dk)
        vs = slice(h * dk, (h + 1) * dk)
        q = q_ref[:, qk]
        k = k_ref[:, qk]
        v = v_ref[:, vs]
        beta = beta_ref[:, h:h + 1]
        e_g = jnp.exp(g_ref[:, h:h + 1])
        s0 = s0_ref[h]
        kb = k * beta
        lhs = jnp.where(row < 4, jnp.broadcast_to(kb * e_g, (8, dk)), jnp.broadcast_to(q * e_g, (8, dk)))
        prod = _dot_hi(lhs, s0)
        v_new = v * beta - prod[0:1, :]
        o = prod[4:5, :] + jnp.sum(q * k, axis=-1, keepdims=True) * v_new
        k8 = jnp.where(row == 0, jnp.broadcast_to(k, (8, dk)), 0.0)
        v8 = jnp.broadcast_to(v_new, (8, dk))
        s_ref[h] = s0 * e_g + _dot_tn_hi(k8, v8)
        z = z_ref[:, vs]
        o_ref[:, vs] = _rms(o, gain_ref[...]) * (z * _sigmoid(z))


def gdn_step(q, k, v, z, beta, g, gain, s0):
    db = q.shape[0]
    n_heads = s0.shape[1]
    vec = lambda w: pl.BlockSpec((None, 1, w), lambda b: (b, 0, 0))
    st = pl.BlockSpec((None, n_heads, GDN_HEAD_DIM, GDN_HEAD_DIM), lambda b: (b, 0, 0, 0))
    return pl.pallas_call(
        _gdn_step_kernel,
        grid=(db,),
        in_specs=[vec(q.shape[2]), vec(q.shape[2]), vec(v.shape[2]), vec(v.shape[2]), vec(n_heads), vec(n_heads),
                  pl.BlockSpec((1, GDN_HEAD_DIM), lambda b: (0, 0)), st],
        out_specs=[vec(v.shape[2]), st],
        out_shape=[jax.ShapeDtypeStruct(v.shape, F32), jax.ShapeDtypeStruct(s0.shape, F32)],
        compiler_params=_params("parallel"),
    )(q, k, v, z, beta, g, gain.reshape(1, GDN_HEAD_DIM), s0)


def _rwkv_post(y, r, k, v, gate, rk, lnw, lnb):
    mean = jnp.mean(y, axis=-1, keepdims=True)
    var = jnp.mean(jnp.square(y - mean), axis=-1, keepdims=True)
    y = (y - mean) * lax.rsqrt(var + RWKV_GN_EPS) * lnw + lnb
    y = y + jnp.sum(r * k * rk, axis=-1, keepdims=True) * v
    return y * gate


def _rwkv_keys(k, a, kk_w, ka_w):
    kk = k * kk_w
    kk = kk * lax.rsqrt(jnp.sum(kk * kk, axis=-1, keepdims=True) + 1e-6)
    k = k * (1.0 + (a - 1.0) * ka_w)
    return k, -kk, kk * a


def _rwkv_chunk_kernel(r_ref, lw_ref, k_ref, v_ref, a_ref, gate_ref, kk_ref, ka_ref, rk_ref, lnw_ref, lnb_ref,
                       o_ref, s_out_ref, s_ref, *, c):
    ch = pl.program_id(2)
    n = RWKV_HEAD_DIM

    @pl.when(ch == 0)
    def _():
        s_ref[...] = jnp.zeros_like(s_ref)

    incl, strict, diag = _tri_masks(c)
    eye = jnp.where(diag, 1.0, 0.0).astype(F32)
    tri = jnp.where(incl, 1.0, 0.0).astype(F32)
    for hh in range(2):
        sl = slice(hh * n, (hh + 1) * n)
        r = r_ref[:, sl]
        lw = lw_ref[:, sl]
        v = v_ref[:, sl]
        k, a_vec, b_vec = _rwkv_keys(k_ref[:, sl], a_ref[:, sl], kk_ref[:, sl], ka_ref[:, sl])
        gi = _dot_hi(tri, lw)
        g_last = gi[c - 1:c, :]
        e_gi = jnp.exp(gi)
        e_neg = jnp.exp(-gi)
        e_rest = jnp.exp(g_last - gi)
        a_t = a_vec * jnp.exp(gi - lw)
        r_t = r * e_gi
        lhs = _bf(jnp.concatenate([a_t, r_t], axis=0))
        rhs = _bf(jnp.concatenate([b_vec * e_neg, k * e_neg], axis=0))
        m4 = _dot_nt(lhs, rhs)
        a_ab = jnp.where(strict, m4[:c, :c], 0.0)
        a_ak = jnp.where(strict, m4[:c, c:], 0.0)
        a_rb = jnp.where(incl, m4[c:, :c], 0.0)
        a_rk = jnp.where(incl, m4[c:, c:], 0.0)
        t_inv = _neumann_inverse(a_ab, eye, c)
        vb = _bf(v)
        sol = _dot_hi(t_inv, jnp.concatenate([a_t, _dot(_bf(a_ak), vb)], axis=-1))
        s = s_ref[hh]
        sb = _bf(s)
        u = _dot_nt(_bf(sol[:, :n]), sb) + sol[:, n:]
        ub = _bf(u)
        y = _dot_nt(_bf(r_t), sb) + _dot(_bf(a_rb), ub) + _dot(_bf(a_rk), vb)
        s_ref[hh] = s * jnp.exp(g_last) + _dot_tn(
            jnp.concatenate([ub, vb], axis=0), _bf(jnp.concatenate([b_vec * e_rest, k * e_rest], axis=0)))
        o_ref[:, sl] = _bf(_rwkv_post(y, r, k, v, gate_ref[:, sl], rk_ref[:, sl], lnw_ref[:, sl], lnb_ref[:, sl]))

    @pl.when(ch == pl.num_programs(2) - 1)
    def _():
        s_out_ref[...] = s_ref[...]


def rwkv_chunk_scan(r, lw, k, v, a, gate, kk_w, ka_w, rk, lnw, lnb, *, batch, seq):
    m, d = r.shape
    n_pairs = d // LANES
    c = min(CHUNK, seq)
    n = seq // c
    tok = pl.BlockSpec((c, LANES), lambda b, h, t: (b * n + t, h))
    par = pl.BlockSpec((1, LANES), lambda b, h, t: (0, h))
    return pl.pallas_call(
        functools.partial(_rwkv_chunk_kernel, c=c),
        grid=(batch, n_pairs, n),
        in_specs=[tok] * 6 + [par] * 5,
        out_specs=[tok, pl.BlockSpec((None, 2, RWKV_HEAD_DIM, RWKV_HEAD_DIM), lambda b, h, t: (b, h, 0, 0))],
        out_shape=[jax.ShapeDtypeStruct((m, d), BF16),
                   jax.ShapeDtypeStruct((batch, d // RWKV_HEAD_DIM, RWKV_HEAD_DIM, RWKV_HEAD_DIM), F32)],
        scratch_shapes=[pltpu.VMEM((2, RWKV_HEAD_DIM, RWKV_HEAD_DIM), F32)],
        compiler_params=_params("parallel", "parallel", "arbitrary"),
    )(r, lw, k, v, a, gate, kk_w, ka_w, rk, lnw, lnb)


def _rwkv_step_kernel(r_ref, lw_ref, k_ref, v_ref, a_ref, gate_ref, kk_ref, ka_ref, rk_ref, lnw_ref, lnb_ref,
                      s0_ref, o_ref, s_ref):
    n = RWKV_HEAD_DIM
    n_heads = s0_ref.shape[0]
    row = lax.broadcasted_iota(jnp.int32, (8, n), 0)
    for h in range(n_heads):
        sl = slice(h * n, (h + 1) * n)
        r = r_ref[:, sl]
        v = v_ref[:, sl]
        k, a_vec, b_vec = _rwkv_keys(k_ref[:, sl], a_ref[:, sl], kk_ref[:, sl], ka_ref[:, sl])
        s0 = s0_ref[h]
        sa = _dot_nt_hi(jnp.broadcast_to(a_vec, (8, n)), s0)[0:1, :]
        left = jnp.where(row == 0, jnp.broadcast_to(sa, (8, n)), jnp.where(row == 1, jnp.broadcast_to(v, (8, n)), 0.0))
        right = jnp.where(row == 0, jnp.broadcast_to(b_vec, (8, n)), jnp.broadcast_to(k, (8, n)))
        s = s0 * jnp.exp(lw_ref[:, sl]) + _dot_tn_hi(left, right)
        s_ref[h] = s
        y = _dot_nt_hi(jnp.broadcast_to(r, (8, n)), s)[0:1, :]
        o_ref[:, sl] = _rwkv_post(y, r, k, v, gate_ref[:, sl], rk_ref[:, sl], lnw_ref[:, sl], lnb_ref[:, sl])


def rwkv_step(r, lw, k, v, a, gate, kk_w, ka_w, rk, lnw, lnb, s0):
    db, _, d = r.shape
    n_heads = s0.shape[1]
    tok = pl.BlockSpec((None, 1, d), lambda b: (b, 0, 0))
    par = pl.BlockSpec((1, d), lambda b: (0, 0))
    st = pl.BlockSpec((None, n_heads, RWKV_HEAD_DIM, RWKV_HEAD_DIM), lambda b: (b, 0, 0, 0))
    return pl.pallas_call(
        _rwkv_step_kernel,
        grid=(db,),
        in_specs=[tok] * 6 + [par] * 5 + [st],
        out_specs=[tok, st],
        out_shape=[jax.ShapeDtypeStruct((db, 1, d), F32), jax.ShapeDtypeStruct(s0.shape, F32)],
        compiler_params=_params("parallel"),
    )(r, lw, k, v, a, gate, kk_w, ka_w, rk, lnw, lnb, s0)


def _head_rms(x, gain, n_heads):
    m = x.shape[0]
    xh = x.reshape(m, n_heads, -1)
    return (xh * lax.rsqrt(jnp.mean(xh * xh, axis=-1, keepdims=True) + NORM_EPS) * gain).reshape(m, -1)


def _fox_project(x, gain, w_in, b_f, q_gain, k_gain):
    width = (w_in.shape[1] // LANES) * LANES
    n_heads = w_in.shape[1] - width
    w = _bf(jnp.concatenate([w_in[:, :width], _pad_cols(w_in[:, width:], LANES)], axis=1))
    h = mm(x, w, gain=gain)
    d = width // 4
    q = _head_rms(h[:, :d], q_gain, n_heads) * (FOX_HEAD_DIM ** -0.5)
    k = _head_rms(h[:, d:2 * d], k_gain, n_heads)
    v = h[:, 2 * d:3 * d]
    gate = _sigmoid(h[:, 3 * d:4 * d])
    log_f = jax.nn.log_sigmoid(h[:, width:width + n_heads] + b_f)
    return q, k, v, gate, log_f


def fox_layer(hp, hs, gain, w_in, b_f, q_gain, k_gain, w_out, layer_idx, cache_k, cache_v, cache_logf, page_table,
              batch, seq):
    n_heads = b_f.shape[0]
    dh = FOX_HEAD_DIM
    w_out_b = _bf(w_out)
    q, k, v, gate, lf = _fox_project(hp, gain, w_in, b_f, q_gain, k_gain)
    cum = jnp.cumsum(lf.reshape(batch, seq, n_heads), axis=1)
    cc = cum.reshape(batch * seq, n_heads // 2, 2).transpose(1, 0, 2)
    tq = min(seq, FOX_Q_TILE)
    cr = cum.reshape(batch, seq // tq, tq, n_heads // 2, 2).transpose(0, 3, 1, 4, 2)
    og = fox_flash(_bf(q), _bf(k), _bf(v), cc, cr, gate, batch=batch, seq=seq)
    hp = mm(og, w_out_b, res=hp)
    outs_p = (k.reshape(batch, seq, n_heads, dh), v.reshape(batch, seq, n_heads, dh), lf.reshape(batch, seq, n_heads))
    db = hs.shape[0]
    q, k, v, gate, lf = _fox_project(hs, gain, w_in, b_f, q_gain, k_gain)
    n_pages = page_table.shape[1]
    page = cache_k.shape[2]
    lf_past = cache_logf[layer_idx][page_table].reshape(db, n_pages * page, n_heads)
    suffix = jnp.flip(jnp.cumsum(jnp.flip(lf_past, axis=1), axis=1), axis=1) - lf_past
    bias = (suffix + lf[:, None, :]).reshape(db, n_pages, 1, page * n_heads)
    o = fox_decode(layer_idx, page_table, q.reshape(db, n_heads, dh), k.reshape(db, n_heads, dh),
                   v.reshape(db, n_heads, dh), bias, cache_k, cache_v)
    og = o.reshape(db, n_heads * dh) * gate
    hs = mm(og, w_out_b, res=hs)
    outs_s = (k.reshape(db, 1, n_heads, dh), v.reshape(db, 1, n_heads, dh), lf.reshape(db, 1, n_heads))
    return hp, hs, outs_p, outs_s


def _gdn_project(x, gain, w_in, n_heads):
    total = w_in.shape[1]
    main = total - 2 * n_heads
    w = _bf(jnp.concatenate([w_in[:, :main], _pad_cols(w_in[:, main:], LANES)], axis=1))
    h = mm(x, w, gain=gain)
    conv_ch = main - n_heads * GDN_HEAD_DIM
    return h[:, :conv_ch], h[:, conv_ch:main], h[:, main:main + n_heads], h[:, main + n_heads:main + 2 * n_heads]


def _gdn_post_conv(conv, a, beta_logit, a_log, dt_bias, n_heads):
    m = conv.shape[0]
    conv = conv * _sigmoid(conv)
    qk_w = (n_heads // 2) * GDN_HEAD_DIM

    def l2(x):
        xh = x.reshape(m, n_heads // 2, GDN_HEAD_DIM)
        return (xh * lax.rsqrt(jnp.sum(xh * xh, axis=-1, keepdims=True) + 1e-6)).reshape(m, qk_w)

    q = l2(conv[:, :qk_w]) * (GDN_HEAD_DIM ** -0.5)
    k = l2(conv[:, qk_w:2 * qk_w])
    v = conv[:, 2 * qk_w:]
    beta = _sigmoid(beta_logit)
    g = -jnp.exp(a_log) * jax.nn.softplus(a + dt_bias)
    return q, k, v, beta, g


def gdn_layer(hp, hs, gain, w_in, conv_w, a_log, dt_bias, out_gain, w_out, conv_state, s_state, batch, seq):
    n_heads = a_log.shape[0]
    w_out_b = _bf(w_out)
    qkv, z, a, bl = _gdn_project(hp, gain, w_in, n_heads)
    ch = qkv.shape[1]
    xc = jnp.concatenate([jnp.zeros((batch, GDN_CONV_W - 1, ch), F32), qkv.reshape(batch, seq, ch)], axis=1)
    conv = xc[:, 0:seq] * conv_w[0]
    for i in range(1, GDN_CONV_W):
        conv = conv + xc[:, i:i + seq] * conv_w[i]
    q, k, v, beta, g = _gdn_post_conv(conv.reshape(batch * seq, ch), a, bl, a_log, dt_bias, n_heads)
    og, s_p = gdn_chunk_scan(q, k, v, z, beta, g, out_gain, batch=batch, seq=seq)
    hp = mm(og, w_out_b, res=hp)
    conv_p = xc[:, seq:]
    db = hs.shape[0]
    qkv, z, a, bl = _gdn_project(hs, gain, w_in, n_heads)
    xc = jnp.concatenate([conv_state, qkv[:, None, :]], axis=1)
    conv = xc[:, 0] * conv_w[0]
    for i in range(1, GDN_CONV_W):
        conv = conv + xc[:, i] * conv_w[i]
    q, k, v, beta, g = _gdn_post_conv(conv, a, bl, a_log, dt_bias, n_heads)
    og, s_s = gdn_step(q[:, None], k[:, None], v[:, None], z[:, None], beta[:, None], g[:, None], out_gain, s_state)
    hs = mm(og.reshape(db, -1), w_out_b, res=hs)
    conv_s = xc[:, 1:]
    return hp, hs, (s_p, conv_p), (s_s, conv_s)


def _rwkv_project(u, u_prev, mu, w0, w1, w2, a0, a1, a2, g1, g2, w_r, w_k, w_v):
    du = u_prev - u
    xr, xw, xk, xv, xa, xg = (u + du * mu[i] for i in range(6))
    lora = lambda w: -(-w.shape[1] // LANES) * LANES
    r = mm(xr, _bf(w_r))
    k = mm(xk, _bf(w_k))
    v = mm(xv, _bf(w_v))
    nw, na, ng = lora(w1), lora(a1), lora(g1)
    w_l = mm(jnp.tanh(mm(xw, _bf(_pad_cols(w1, nw)))), _bf(_pad_rows(w2, nw)))
    a_l = mm(mm(xa, _bf(_pad_cols(a1, na))), _bf(_pad_rows(a2, na)))
    gate = mm(_sigmoid(mm(xg, _bf(_pad_cols(g1, ng)))), _bf(_pad_rows(g2, ng)))
    w_raw = -jax.nn.softplus(-(w0 + w_l)) - 0.5
    log_decay = -jnp.exp(w_raw)
    a = _sigmoid(a0 + a_l)
    return r, log_decay, k, v, a, gate


def rwkv_layer(hp, hs, gain, mu, w0, w1, w2, a0, a1, a2, g1, g2, k_k, k_a, r_k, w_r, w_k, w_v, w_o, ln_w, ln_b,
               shift_state, wkv_state, batch, seq):
    d = hp.shape[1]
    w_o_b = _bf(w_o)
    proj_w = (mu, w0, w1, w2, a0, a1, a2, g1, g2, w_r, w_k, w_v)
    chan = (k_k.reshape(1, d), k_a.reshape(1, d), r_k.reshape(1, d), ln_w.reshape(1, d), ln_b.reshape(1, d))
    u = _rms(hp, gain).reshape(batch, seq, d)
    u_prev = jnp.concatenate([jnp.zeros((batch, 1, d), F32), u[:, :-1]], axis=1)
    toks = _rwkv_project(u.reshape(batch * seq, d), u_prev.reshape(batch * seq, d), *proj_w)
    og, s_p = rwkv_chunk_scan(*toks, *chan, batch=batch, seq=seq)
    hp = mm(og, w_o_b, res=hp)
    shift_p = u[:, -1]
    db = hs.shape[0]
    us = _rms(hs, gain)
    toks = _rwkv_project(us, shift_state, *proj_w)
    og, s_s = rwkv_step(*(t[:, None] for t in toks), *chan, wkv_state)
    hs = mm(og.reshape(db, d), w_o_b, res=hs)
    return hp, hs, (shift_p, s_p), (us, s_s)


def kernel(x_prompt, x_sample, cache_k, cache_v, cache_logf, page_table, state_gdn_s, state_gdn_conv, state_rwkv_shift, state_rwkv_wkv, norm_mix, norm_ffn, norm_final, fox_w_in, fox_b_f, fox_q_norm, fox_k_norm, fox_w_out, gdn_w_in, gdn_conv_w, gdn_a_log, gdn_dt_bias, gdn_out_norm, gdn_w_out, rwkv_mu, rwkv_w0, rwkv_w1, rwkv_w2, rwkv_a0, rwkv_a1, rwkv_a2, rwkv_g1, rwkv_g2, rwkv_k_k, rwkv_k_a, rwkv_r_k, rwkv_w_r, rwkv_w_k, rwkv_w_v, rwkv_w_o, rwkv_ln_w, rwkv_ln_b, ffn_w_in, ffn_w_out):
    batch, seq, d = x_prompt.shape
    db = x_sample.shape[0]
    depth = norm_mix.shape[0]
    hp = x_prompt.reshape(batch * seq, d)
    hs = x_sample.reshape(db, d)
    fox_p, fox_s, gdn_p, gdn_s, rwkv_p, rwkv_s = [], [], [], [], [], []
    for layer in range(depth):
        kind, j = layer % 3, layer // 3
        if kind == 0:
            hp, hs, op, os_ = fox_layer(hp, hs, norm_mix[layer], fox_w_in[j], fox_b_f[j], fox_q_norm[j], fox_k_norm[j],
                                        fox_w_out[j], j, cache_k, cache_v, cache_logf, page_table, batch, seq)
            fox_p.append(op)
            fox_s.append(os_)
        elif kind == 1:
            hp, hs, op, os_ = gdn_layer(hp, hs, norm_mix[layer], gdn_w_in[j], gdn_conv_w[j], gdn_a_log[j],
                                        gdn_dt_bias[j], gdn_out_norm[j], gdn_w_out[j], state_gdn_conv[j],
                                        state_gdn_s[j], batch, seq)
            gdn_p.append(op)
            gdn_s.append(os_)
        else:
            hp, hs, op, os_ = rwkv_layer(hp, hs, norm_mix[layer], rwkv_mu[j], rwkv_w0[j], rwkv_w1[j], rwkv_w2[j],
                                         rwkv_a0[j], rwkv_a1[j], rwkv_a2[j], rwkv_g1[j], rwkv_g2[j], rwkv_k_k[j],
                                         rwkv_k_a[j], rwkv_r_k[j], rwkv_w_r[j], rwkv_w_k[j], rwkv_w_v[j],
                                         rwkv_w_o[j], rwkv_ln_w[j], rwkv_ln_b[j], state_rwkv_shift[j],
                                         state_rwkv_wkv[j], batch, seq)
            rwkv_p.append(op)
            rwkv_s.append(os_)
        final = norm_final if layer == depth - 1 else None
        w_in_b, w_out_b = _bf(ffn_w_in[layer]), _bf(ffn_w_out[layer])
        hp = ffn(hp, norm_ffn[layer], w_in_b, w_out_b, final)
        hs = ffn(hs, norm_ffn[layer], w_in_b, w_out_b, final)
    stack = lambda items, i: jnp.stack([it[i] for it in items])
    return (hp.reshape(batch, seq, d), hs.reshape(db, 1, d),
            stack(fox_p, 0), stack(fox_p, 1), stack(fox_p, 2),
            stack(fox_s, 0), stack(fox_s, 1), stack(fox_s, 2),
            stack(gdn_p, 0), stack(gdn_p, 1), stack(gdn_s, 0), stack(gdn_s, 1),
            stack(rwkv_p, 0), stack(rwkv_p, 1), stack(rwkv_s, 0), stack(rwkv_s, 1))
```

```python
import functools

import jax
import jax.numpy as jnp
from jax import lax
from jax.experimental import pallas as pl
from jax.experimental.pallas import tpu as pltpu

F32 = jnp.float32
BF16 = jnp.bfloat16
HIGHEST = lax.Precision.HIGHEST

NORM_EPS = 1e-6
RWKV_GN_EPS = 64e-5
LANES = 128
VMEM_LIMIT_BYTES = 48 * 1024 * 1024
NEG_BIG = -1e30

FOX_HEAD_DIM = 64
GDN_HEAD_DIM = 128
GDN_CONV_W = 4
RWKV_HEAD_DIM = 64
CHUNK = 64
FOX_Q_TILE = 256
FOX_K_TILE = 512


def _params(*semantics):
    return pltpu.CompilerParams(dimension_semantics=semantics, vmem_limit_bytes=VMEM_LIMIT_BYTES)


def _sigmoid(x):
    return 1.0 / (1.0 + jnp.exp(-x))


def _rms(x, gain):
    return x * lax.rsqrt(jnp.mean(x * x, axis=-1, keepdims=True) + NORM_EPS) * gain


def _dot(a, b):
    return jnp.dot(a, b, preferred_element_type=F32)


def _dot_nt(a, b):
    return lax.dot_general(a, b, (((1,), (1,)), ((), ())), preferred_element_type=F32)


def _dot_tn(a, b):
    return lax.dot_general(a, b, (((0,), (0,)), ((), ())), preferred_element_type=F32)


def _dot_hi(a, b):
    return jnp.dot(a, b, preferred_element_type=F32, precision=HIGHEST)


def _dot_nt_hi(a, b):
    return lax.dot_general(a, b, (((1,), (1,)), ((), ())), preferred_element_type=F32, precision=HIGHEST)


def _dot_tn_hi(a, b):
    return lax.dot_general(a, b, (((0,), (0,)), ((), ())), preferred_element_type=F32, precision=HIGHEST)


def _bf(x):
    return x.astype(BF16)


def _top_bits(x):
    bits = lax.bitcast_convert_type(x, jnp.int32) & jnp.int32(-65536)
    return lax.bitcast_convert_type(bits, F32)


def _split2(x):
    hi = _top_bits(x)
    return _bf(hi), _bf(x - hi)


def _split3(x):
    hi = _top_bits(x)
    r = x - hi
    mid = _top_bits(r)
    return _bf(hi), _bf(mid), _bf(r - mid)


def _dot_x3(a2, b2):
    (ah, al), (bh, bl) = a2, b2
    return _dot(ah, bh) + (_dot(ah, bl) + _dot(al, bh))


def _pad_cols(w, n):
    return jnp.pad(w, ((0, 0), (0, n - w.shape[1])))


def _pad_rows(w, n):
    return jnp.pad(w, ((0, n - w.shape[0]), (0, 0)))


def _col_tile(n, cap=1536):
    best = LANES
    for t in range(LANES, min(n, cap) + 1, LANES):
        if n % t == 0:
            best = t
    return best


def _mm_kernel(*refs, norm, residual):
    it = iter(refs)
    x_ref = next(it)
    g_ref = next(it) if norm else None
    w_ref = next(it)
    r_ref = next(it) if residual else None
    o_ref = next(it)
    xn_ref = next(it) if norm else None
    if norm:
        @pl.when(pl.program_id(1) == 0)
        def _():
            xn_ref[...] = _bf(_rms(x_ref[...], g_ref[...]))
        a = xn_ref[...]
    else:
        a = _bf(x_ref[...])
    acc = _dot(a, w_ref[...])
    if residual:
        acc = acc + r_ref[...]
    o_ref[...] = acc.astype(o_ref.dtype)


def mm(x, w, *, gain=None, res=None, out_dtype=F32):
    m, k = x.shape
    n = w.shape[1]
    tm = min(m, 512)
    tn = _col_tile(n)
    norm, residual = gain is not None, res is not None
    in_specs = [pl.BlockSpec((tm, k), lambda i, j: (i, 0))]
    args = [x]
    if norm:
        in_specs.append(pl.BlockSpec((1, k), lambda i, j: (0, 0)))
        args.append(gain.reshape(1, k))
    in_specs.append(pl.BlockSpec((k, tn), lambda i, j: (0, j)))
    args.append(w)
    if residual:
        in_specs.append(pl.BlockSpec((tm, tn), lambda i, j: (i, j)))
        args.append(res)
    return pl.pallas_call(
        functools.partial(_mm_kernel, norm=norm, residual=residual),
        grid=(m // tm, n // tn),
        in_specs=in_specs,
        out_specs=pl.BlockSpec((tm, tn), lambda i, j: (i, j)),
        out_shape=jax.ShapeDtypeStruct((m, n), out_dtype),
        scratch_shapes=[pltpu.VMEM((tm, k), BF16)] if norm else [],
        compiler_params=_params("parallel", "arbitrary"),
        name="dense",
    )(*args)


def _ffn_kernel(*refs, final):
    if final:
        x_ref, g_ref, wg_ref, wu_ref, wo_ref, fg_ref, o_ref, xn_ref, acc_ref = refs
    else:
        x_ref, g_ref, wg_ref, wu_ref, wo_ref, o_ref, xn_ref, acc_ref = refs
    k = pl.program_id(1)

    @pl.when(k == 0)
    def _():
        xn_ref[...] = _bf(_rms(x_ref[...], g_ref[...]))
        acc_ref[...] = jnp.zeros_like(acc_ref)

    xn = xn_ref[...]
    gate = _dot(xn, wg_ref[...])
    up = _dot(xn, wu_ref[...])
    act = _bf(gate * _sigmoid(gate) * up)
    acc_ref[...] += _dot(act, wo_ref[...])

    @pl.when(k == pl.num_programs(1) - 1)
    def _():
        out = x_ref[...] + acc_ref[...]
        if final:
            out = _rms(out, fg_ref[...])
        o_ref[...] = out


def ffn(x, gain, w_in, w_out, final_gain=None):
    m, d = x.shape
    hidden = w_out.shape[0]
    th = 256
    nk = hidden // th
    tm = min(m, 512)
    final = final_gain is not None
    in_specs = [
        pl.BlockSpec((tm, d), lambda i, k: (i, 0)),
        pl.BlockSpec((1, d), lambda i, k: (0, 0)),
        pl.BlockSpec((d, th), lambda i, k: (0, k)),
        pl.BlockSpec((d, th), lambda i, k: (0, k + nk)),
        pl.BlockSpec((th, d), lambda i, k: (k, 0)),
    ]
    args = [x, gain.reshape(1, d), w_in, w_in, w_out]
    if final:
        in_specs.append(pl.BlockSpec((1, d), lambda i, k: (0, 0)))
        args.append(final_gain.reshape(1, d))
    return pl.pallas_call(
        functools.partial(_ffn_kernel, final=final),
        grid=(m // tm, nk),
        in_specs=in_specs,
        out_specs=pl.BlockSpec((tm, d), lambda i, k: (i, 0)),
        out_shape=jax.ShapeDtypeStruct((m, d), F32),
        scratch_shapes=[pltpu.VMEM((tm, d), BF16), pltpu.VMEM((tm, d), F32)],
        compiler_params=_params("parallel", "arbitrary"),
        name="swiglu",
    )(*args)


def _fox_flash_kernel(q_ref, k_ref, vt_ref, g_ref, o_ref, m_ref, acc_ref, *, tq, tk):
    i = pl.program_id(2)
    dh = FOX_HEAD_DIM
    m_ref[...] = jnp.full_like(m_ref, NEG_BIG)
    acc_ref[...] = jnp.zeros_like(acc_ref)
    n_full = (i * tq) // tk
    key = lax.broadcasted_iota(jnp.int32, (tk, tq), 0)
    qry = lax.broadcasted_iota(jnp.int32, (tk, tq), 1)

    def block(j, masked):
        off = pl.multiple_of(j * tk, tk)
        heads = range(2)
        ss = [_dot(k_ref[pl.ds(off, tk), hh * LANES:(hh + 1) * LANES], q_ref[hh]) for hh in heads]
        if masked:
            ss = [jnp.where(key + off <= qry + i * tq, s, NEG_BIG) for s in ss]
        m_prev = [m_ref[hh] for hh in heads]
        m_new = [jnp.maximum(m_prev[hh], jnp.max(ss[hh], axis=0, keepdims=True)) for hh in heads]
        ps = [_bf(jnp.exp(ss[hh] - m_new[hh])) for hh in heads]
        pv = [_dot(vt_ref[j, hh], ps[hh]) for hh in heads]
        for hh in heads:
            acc_ref[hh] = jnp.exp(m_prev[hh] - m_new[hh]) * acc_ref[hh] + pv[hh]
            m_ref[hh] = m_new[hh]

    def body(j, carry):
        block(j, False)
        return carry

    lax.fori_loop(0, n_full, body, 0)
    block(n_full, True)
    halves = []
    for hh in range(2):
        a = acc_ref[hh].T
        halves.append(a[:, :dh] / a[:, dh:dh + 1])
    o_ref[...] = _bf(jnp.concatenate(halves, axis=-1) * g_ref[...])


def fox_flash(qt, ka, vt, gate, *, batch, seq):
    m = ka.shape[0]
    n_pairs, nk, tk = vt.shape[1], vt.shape[2], vt.shape[5]
    tq = min(seq, FOX_Q_TILE)
    nq = seq // tq
    return pl.pallas_call(
        functools.partial(_fox_flash_kernel, tq=tq, tk=tk),
        grid=(batch, n_pairs, nq),
        in_specs=[
            pl.BlockSpec((None, None, 2, LANES, tq), lambda b, h, i: (b, h, 0, 0, i)),
            pl.BlockSpec((seq, 2 * LANES), lambda b, h, i: (b, h)),
            pl.BlockSpec((None, None, nk, 2, LANES, tk), lambda b, h, i: (b, h, 0, 0, 0, 0)),
            pl.BlockSpec((tq, LANES), lambda b, h, i: (b * nq + i, h)),
        ],
        out_specs=pl.BlockSpec((tq, LANES), lambda b, h, i: (b * nq + i, h)),
        out_shape=jax.ShapeDtypeStruct((m, n_pairs * LANES), BF16),
        scratch_shapes=[pltpu.VMEM((2, 1, tq), F32), pltpu.VMEM((2, LANES, tq), F32)],
        compiler_params=_params("parallel", "parallel", "arbitrary"),
        name="fox_flash",
    )(qt, ka, vt, gate)


def _fox_decode_kernel(pt_ref, q_ref, kn_ref, vn_ref, lfn_ref, g_ref, lfc_ref, kc_ref, vc_ref, o_ref,
                       qb_ref, m_ref, l_ref, carry_ref, acc_ref):
    p = pl.program_id(1)
    n_heads, dh, page = kc_ref.shape

    @pl.when(p == 0)
    def _():
        eye = (lax.broadcasted_iota(jnp.int32, (dh, dh), 0) == lax.broadcasted_iota(jnp.int32, (dh, dh), 1)).astype(F32)
        q_t = _dot_nt_hi(eye, q_ref[...])
        kn_t = _dot_nt_hi(eye, kn_ref[...])
        vn_t = _dot_nt_hi(eye, vn_ref[...])
        lane0 = lax.broadcasted_iota(jnp.int32, (dh, page), 1) == 0
        for h in range(n_heads):
            q_col = q_t[:, h:h + 1]
            qb_ref[h] = jnp.broadcast_to(q_col, (dh, page))
            s_new = jnp.sum(q_col * kn_t[:, h:h + 1], axis=0, keepdims=True)
            m_ref[h:h + 1, :] = jnp.broadcast_to(s_new, (1, page))
            acc_ref[h] = jnp.where(lane0, jnp.broadcast_to(vn_t[:, h:h + 1], (dh, page)), 0.0)
        l_ref[...] = jnp.ones_like(l_ref)
        carry_ref[...] = jnp.broadcast_to(lfn_ref[...], (n_heads, page))

    lf = lfc_ref[...]
    later = (lax.broadcasted_iota(jnp.int32, (page, page), 0) > lax.broadcasted_iota(jnp.int32, (page, page), 1)).astype(F32)
    bias = carry_ref[...] + _dot_hi(lf, later)
    carry_ref[...] = carry_ref[...] + jnp.sum(lf, axis=-1, keepdims=True)
    s = jnp.concatenate([jnp.sum(qb_ref[h] * kc_ref[h], axis=0, keepdims=True) for h in range(n_heads)], axis=0)
    s = s + bias
    m_prev = m_ref[...]
    m_new = jnp.maximum(m_prev, jnp.max(s, axis=-1, keepdims=True))
    alpha = jnp.exp(m_prev - m_new)
    pr = jnp.exp(s - m_new)
    l_ref[...] = alpha * l_ref[...] + jnp.sum(pr, axis=-1, keepdims=True)
    m_ref[...] = m_new
    for h in range(n_heads):
        acc_ref[h] = alpha[h:h + 1, :] * acc_ref[h] + pr[h:h + 1, :] * vc_ref[h]

    @pl.when(p == pl.num_programs(1) - 1)
    def _():
        for h in range(n_heads):
            acc_ref[h] = acc_ref[h] / l_ref[h:h + 1, :]
        ones = jnp.ones((8, page), F32)
        o = _dot_nt_hi(ones, acc_ref[...].reshape(n_heads * dh, page))
        o_ref[...] = o[0:1, :] * g_ref[...]


def fox_decode(layer_idx, page_table, q, k_new, v_new, lf_new, gate, cache_logf_t, cache_k_t, cache_v_t):
    db, n_heads, dh = q.shape
    n_pages = page_table.shape[1]
    page = cache_k_t.shape[4]
    vec = pl.BlockSpec((None, n_heads, dh), lambda b, p, pt: (b, 0, 0))
    row = pl.BlockSpec((None, 1, n_heads * dh), lambda b, p, pt: (b, 0, 0))
    past = lambda b, p, pt: pt[b, n_pages - 1 - p]
    cache = pl.BlockSpec((None, None, n_heads, dh, page), lambda b, p, pt: (layer_idx, past(b, p, pt), 0, 0, 0))
    return pl.pallas_call(
        _fox_decode_kernel,
        grid_spec=pltpu.PrefetchScalarGridSpec(
            num_scalar_prefetch=1,
            grid=(db, n_pages),
            in_specs=[vec, vec, vec,
                      pl.BlockSpec((None, n_heads, 1), lambda b, p, pt: (b, 0, 0)),
                      row,
                      pl.BlockSpec((None, None, n_heads, page), lambda b, p, pt: (layer_idx, past(b, p, pt), 0, 0)),
                      cache, cache],
            out_specs=row,
            scratch_shapes=[pltpu.VMEM((n_heads, dh, page), F32), pltpu.VMEM((n_heads, page), F32),
                            pltpu.VMEM((n_heads, page), F32), pltpu.VMEM((n_heads, page), F32),
                            pltpu.VMEM((n_heads, dh, page), F32)],
        ),
        out_shape=jax.ShapeDtypeStruct((db, 1, n_heads * dh), F32),
        compiler_params=_params("parallel", "arbitrary"),
        name="fox_decode",
    )(page_table, q, k_new, v_new, lf_new, gate, cache_logf_t, cache_k_t, cache_v_t)


def _tri_masks(c):
    row = lax.broadcasted_iota(jnp.int32, (c, c), 0)
    col = lax.broadcasted_iota(jnp.int32, (c, c), 1)
    return row >= col, row > col, row == col


def _cumsum_rows(tri_b, x):
    hi, mid, lo = _split3(x)
    return _dot(tri_b, hi) + (_dot(tri_b, mid) + _dot(tri_b, lo))


def _neumann_inverses(mats, eye, c):
    steps = max(c.bit_length() - 2, 0)
    ps = [eye + a for a in mats]
    if steps == 0:
        return ps
    splits = [_split2(a) for a in mats]
    aks = [_dot_x3(s, s) for s in splits]
    for step in range(steps):
        last = step == steps - 1
        nxt = []
        for p, ak in zip(ps, aks):
            ak2 = _split2(ak)
            lhs = p if last else jnp.concatenate([p, ak], axis=0)
            nxt.append(_dot_x3(_split2(lhs), ak2))
        ps = [p + n[:c] for p, n in zip(ps, nxt)]
        aks = [None if last else n[c:] for n in nxt]
    return ps


def _gdn_chunk_kernel(q_ref, k_ref, v_ref, z_ref, bcol_ref, gcol_ref, grow_ref, gain_ref, o_ref, s_out_ref, s_ref, *, c, hb):
    ch = pl.program_id(2)
    dk = GDN_HEAD_DIM

    @pl.when(ch == 0)
    def _():
        s_ref[...] = jnp.zeros_like(s_ref)

    incl, strict, diag = _tri_masks(c)
    eye = jnp.where(diag, 1.0, 0.0).astype(F32)
    tri_b = jnp.where(incl, 1.0, 0.0).astype(BF16)
    gc_col = _cumsum_rows(tri_b, gcol_ref[...])
    g3 = _split3(grow_ref[...])
    gc_row = _dot_nt(g3[0], tri_b) + (_dot_nt(g3[1], tri_b) + _dot_nt(g3[2], tri_b))
    heads = range(hb)
    vs = [slice(hh * dk, (hh + 1) * dk) for hh in heads]
    q = [q_ref[:, (hh // 2) * dk:(hh // 2 + 1) * dk] for hh in heads]
    k = [k_ref[:, (hh // 2) * dk:(hh // 2 + 1) * dk] for hh in heads]
    beta = [bcol_ref[:, hh:hh + 1] for hh in heads]
    gcc = [gc_col[:, hh:hh + 1] for hh in heads]
    g_last = [g[c - 1:c, :] for g in gcc]
    decay = [jnp.where(incl, jnp.exp(jnp.where(incl, gcc[hh] - gc_row[hh:hh + 1, :], 0.0)), 0.0) for hh in heads]
    kb = [k[hh] * beta[hh] for hh in heads]
    kbf = [_bf(x) for x in k]
    kk = [_dot_nt(_bf(kb[hh]), kbf[hh]) for hh in heads]
    qk = [_dot_nt(_bf(q[hh]), kbf[hh]) for hh in heads]
    neg_lower = [jnp.where(strict, -(kk[hh] * decay[hh]), 0.0) for hh in heads]
    attn = [_bf(jnp.where(incl, qk[hh] * decay[hh], 0.0)) for hh in heads]
    e_gc = [jnp.exp(g) for g in gcc]
    rhs = [_split2(jnp.concatenate([v_ref[:, vs[hh]] * beta[hh], kb[hh] * e_gc[hh]], axis=-1)) for hh in heads]
    q_dec = [_bf(q[hh] * e_gc[hh]) for hh in heads]
    k_dec = [_bf(k[hh] * jnp.exp(g_last[hh] - gcc[hh])) for hh in heads]
    t_inv = _neumann_inverses(neg_lower, eye, c)
    sol = [_dot_x3(_split2(t_inv[hh]), rhs[hh]) for hh in heads]
    s = [s_ref[hh] for hh in heads]
    sb = [_bf(x) for x in s]
    vnb = [_bf(sol[hh][:, :dk] - _dot(_bf(sol[hh][:, dk:]), sb[hh])) for hh in heads]
    o = [_dot(q_dec[hh], sb[hh]) + _dot(attn[hh], vnb[hh]) for hh in heads]
    s_new = [s[hh] * jnp.exp(g_last[hh]) + _dot_tn(k_dec[hh], vnb[hh]) for hh in heads]
    for hh in heads:
        s_ref[hh] = s_new[hh]
        z = z_ref[:, vs[hh]]
        o_ref[:, vs[hh]] = _bf(_rms(o[hh], gain_ref[...]) * (z * _sigmoid(z)))

    @pl.when(ch == pl.num_programs(2) - 1)
    def _():
        s_out_ref[...] = s_ref[...]


def gdn_chunk_scan(q, k, v, z, beta, g, gain, *, batch, seq):
    m = q.shape[0]
    n_heads = v.shape[1] // GDN_HEAD_DIM
    c = min(CHUNK, seq)
    hb = min(8, n_heads)
    n_groups = n_heads // hb
    n = seq // c
    bcol = beta.reshape(m, n_groups, hb).transpose(1, 0, 2)
    gcol = g.reshape(m, n_groups, hb).transpose(1, 0, 2)
    grow = g.reshape(batch * n, c, n_groups, hb).transpose(0, 2, 3, 1)
    qk_w = (hb // 2) * GDN_HEAD_DIM
    v_w = hb * GDN_HEAD_DIM
    row_blk = lambda b, h, t: (b * n + t, h)
    col_spec = pl.BlockSpec((None, c, hb), lambda b, h, t: (h, b * n + t, 0))
    return pl.pallas_call(
        functools.partial(_gdn_chunk_kernel, c=c, hb=hb),
        grid=(batch, n_groups, n),
        in_specs=[
            pl.BlockSpec((c, qk_w), row_blk),
            pl.BlockSpec((c, qk_w), row_blk),
            pl.BlockSpec((c, v_w), row_blk),
            pl.BlockSpec((c, v_w), row_blk),
            col_spec, col_spec,
            pl.BlockSpec((None, None, hb, c), lambda b, h, t: (b * n + t, h, 0, 0)),
            pl.BlockSpec((1, GDN_HEAD_DIM), lambda b, h, t: (0, 0)),
        ],
        out_specs=[
            pl.BlockSpec((c, v_w), row_blk),
            pl.BlockSpec((None, hb, GDN_HEAD_DIM, GDN_HEAD_DIM), lambda b, h, t: (b, h, 0, 0)),
        ],
        out_shape=[jax.ShapeDtypeStruct((m, n_heads * GDN_HEAD_DIM), BF16),
                   jax.ShapeDtypeStruct((batch, n_heads, GDN_HEAD_DIM, GDN_HEAD_DIM), F32)],
        scratch_shapes=[pltpu.VMEM((hb, GDN_HEAD_DIM, GDN_HEAD_DIM), F32)],
        compiler_params=_params("parallel", "parallel", "arbitrary"),
        name="gdn_chunk_scan",
    )(q, k, v, z, bcol, gcol, grow, gain.reshape(1, GDN_HEAD_DIM))


def _gdn_step_kernel(q_ref, k_ref, v_ref, z_ref, beta_ref, g_ref, gain_ref, s0_ref, o_ref, s_ref):
    dk = GDN_HEAD_DIM
    n_heads = s0_ref.shape[0]
    row = lax.broadcasted_iota(jnp.int32, (8, dk), 0)
    for h in range(n_heads):
        qk = slice((h // 2) * dk, (h // 2 + 1) * dk)
        vs = slice(h * dk, (h + 1) * dk)
        q = q_ref[:, qk]
        k = k_ref[:, qk]
        v = v_ref[:, vs]
        beta = beta_ref[:, h:h + 1]
        e_g = jnp.exp(g_ref[:, h:h + 1])
        s0 = s0_ref[h]
        kb = k * beta
        lhs = jnp.where(row < 4, jnp.broadcast_to(kb * e_g, (8, dk)), jnp.broadcast_to(q * e_g, (8, dk)))
        prod = _dot_hi(lhs, s0)
        v_new = v * beta - prod[0:1, :]
        o = prod[4:5, :] + jnp.sum(q * k, axis=-1, keepdims=True) * v_new
        k8 = jnp.where(row == 0, jnp.broadcast_to(k, (8, dk)), 0.0)
        v8 = jnp.broadcast_to(v_new, (8, dk))
        s_ref[h] = s0 * e_g + _dot_tn_hi(k8, v8)
        z = z_ref[:, vs]
        o_ref[:, vs] = _rms(o, gain_ref[...]) * (z * _sigmoid(z))


def gdn_step(q, k, v, z, beta, g, gain, s0):
    db = q.shape[0]
    n_heads = s0.shape[1]
    vec = lambda w: pl.BlockSpec((None, 1, w), lambda b: (b, 0, 0))
    st = pl.BlockSpec((None, n_heads, GDN_HEAD_DIM, GDN_HEAD_DIM), lambda b: (b, 0, 0, 0))
    return pl.pallas_call(
        _gdn_step_kernel,
        grid=(db,),
        in_specs=[vec(q.shape[2]), vec(q.shape[2]), vec(v.shape[2]), vec(v.shape[2]), vec(n_heads), vec(n_heads),
                  pl.BlockSpec((1, GDN_HEAD_DIM), lambda b: (0, 0)), st],
        out_specs=[vec(v.shape[2]), st],
        out_shape=[jax.ShapeDtypeStruct(v.shape, F32), jax.ShapeDtypeStruct(s0.shape, F32)],
        compiler_params=_params("parallel"),
        name="gdn_step",
    )(q, k, v, z, beta, g, gain.reshape(1, GDN_HEAD_DIM), s0)


def _rwkv_post(y, r, k, v, gate, rk, lnw, lnb):
    mean = jnp.mean(y, axis=-1, keepdims=True)
    var = jnp.mean(jnp.square(y - mean), axis=-1, keepdims=True)
    y = (y - mean) * lax.rsqrt(var + RWKV_GN_EPS) * lnw + lnb
    y = y + jnp.sum(r * k * rk, axis=-1, keepdims=True) * v
    return y * gate


def _rwkv_keys(k, a, kk_w, ka_w):
    kk = k * kk_w
    kk = kk * lax.rsqrt(jnp.sum(kk * kk, axis=-1, keepdims=True) + 1e-6)
    k = k * (1.0 + (a - 1.0) * ka_w)
    return k, -kk, kk * a


def _rwkv_chunk_kernel(r_ref, lw_ref, k_ref, v_ref, a_ref, gate_ref, kk_ref, ka_ref, rk_ref, lnw_ref, lnb_ref,
                       o_ref, s_out_ref, s_ref, *, c, hb):
    ch = pl.program_id(2)
    n = RWKV_HEAD_DIM

    @pl.when(ch == 0)
    def _():
        s_ref[...] = jnp.zeros_like(s_ref)

    incl, strict, diag = _tri_masks(c)
    eye = jnp.where(diag, 1.0, 0.0).astype(F32)
    tri_b = jnp.where(incl, 1.0, 0.0).astype(BF16)
    gi_all = _cumsum_rows(tri_b, lw_ref[...])
    heads = range(hb)
    sl = [slice(hh * n, (hh + 1) * n) for hh in heads]
    r = [r_ref[:, s_] for s_ in sl]
    v = [v_ref[:, s_] for s_ in sl]
    vb = [_bf(x) for x in v]
    keys = [_rwkv_keys(k_ref[:, s_], a_ref[:, s_], kk_ref[:, s_], ka_ref[:, s_]) for s_ in sl]
    k = [t[0] for t in keys]
    a_vec = [t[1] for t in keys]
    b_vec = [t[2] for t in keys]
    gi = [gi_all[:, s_] for s_ in sl]
    g_last = [g[c - 1:c, :] for g in gi]
    e_neg = [jnp.exp(-g) for g in gi]
    e_rest = [jnp.exp(g_last[hh] - gi[hh]) for hh in heads]
    a_t = [a_vec[hh] * jnp.exp(gi[hh] - lw_ref[:, sl[hh]]) for hh in heads]
    r_tb = [_bf(r[hh] * jnp.exp(gi[hh])) for hh in heads]
    lhs = [jnp.concatenate([_bf(a_t[hh]), r_tb[hh]], axis=0) for hh in heads]
    rhs = [_bf(jnp.concatenate([b_vec[hh] * e_neg[hh], k[hh] * e_neg[hh]], axis=0)) for hh in heads]
    tail = [_bf(jnp.concatenate([b_vec[hh] * e_rest[hh], k[hh] * e_rest[hh]], axis=0)) for hh in heads]
    m4 = [_dot_nt(lhs[hh], rhs[hh]) for hh in heads]
    a_ab = [jnp.where(strict, m[:c, :c], 0.0) for m in m4]
    a_ak = [_bf(jnp.where(strict, m[:c, c:], 0.0)) for m in m4]
    a_rb = [_bf(jnp.where(incl, m[c:, :c], 0.0)) for m in m4]
    a_rk = [_bf(jnp.where(incl, m[c:, c:], 0.0)) for m in m4]
    akv = [_dot(a_ak[hh], vb[hh]) for hh in heads]
    t_inv = _neumann_inverses(a_ab, eye, c)
    sol = [_dot_x3(_split2(t_inv[hh]), _split2(jnp.concatenate([a_t[hh], akv[hh]], axis=-1))) for hh in heads]
    s = [s_ref[hh] for hh in heads]
    sb = [_bf(x) for x in s]
    ub = [_bf(_dot_nt(_bf(sol[hh][:, :n]), sb[hh]) + sol[hh][:, n:]) for hh in heads]
    y = [_dot_nt(r_tb[hh], sb[hh]) + _dot(a_rb[hh], ub[hh]) + _dot(a_rk[hh], vb[hh]) for hh in heads]
    s_new = [s[hh] * jnp.exp(g_last[hh]) + _dot_tn(jnp.concatenate([ub[hh], vb[hh]], axis=0), tail[hh])
             for hh in heads]
    for hh in heads:
        s_ref[hh] = s_new[hh]
        o_ref[:, sl[hh]] = _bf(_rwkv_post(y[hh], r[hh], k[hh], v[hh], gate_ref[:, sl[hh]], rk_ref[:, sl[hh]],
                                          lnw_ref[:, sl[hh]], lnb_ref[:, sl[hh]]))

    @pl.when(ch == pl.num_programs(2) - 1)
    def _():
        s_out_ref[...] = s_ref[...]


def rwkv_chunk_scan(r, lw, k, v, a, gate, kk_w, ka_w, rk, lnw, lnb, *, batch, seq):
    m, d = r.shape
    n_heads = d // RWKV_HEAD_DIM
    hb = min(8, n_heads)
    w = hb * RWKV_HEAD_DIM
    c = min(CHUNK, seq)
    n = seq // c
    tok = pl.BlockSpec((c, w), lambda b, h, t: (b * n + t, h))
    par = pl.BlockSpec((1, w), lambda b, h, t: (0, h))
    return pl.pallas_call(
        functools.partial(_rwkv_chunk_kernel, c=c, hb=hb),
        grid=(batch, n_heads // hb, n),
        in_specs=[tok] * 6 + [par] * 5,
        out_specs=[tok, pl.BlockSpec((None, hb, RWKV_HEAD_DIM, RWKV_HEAD_DIM), lambda b, h, t: (b, h, 0, 0))],
        out_shape=[jax.ShapeDtypeStruct((m, d), BF16),
                   jax.ShapeDtypeStruct((batch, n_heads, RWKV_HEAD_DIM, RWKV_HEAD_DIM), F32)],
        scratch_shapes=[pltpu.VMEM((hb, RWKV_HEAD_DIM, RWKV_HEAD_DIM), F32)],
        compiler_params=_params("parallel", "parallel", "arbitrary"),
        name="rwkv_chunk_scan",
    )(r, lw, k, v, a, gate, kk_w, ka_w, rk, lnw, lnb)


def _rwkv_step_kernel(r_ref, lw_ref, k_ref, v_ref, a_ref, gate_ref, kk_ref, ka_ref, rk_ref, lnw_ref, lnb_ref,
                      s0_ref, o_ref, s_ref, y_ref):
    n = s0_ref.shape[0]
    r = r_ref[...]
    v = v_ref[...]
    k0 = k_ref[...]
    a = a_ref[...]
    kk = k0 * kk_ref[...]
    kk = kk * lax.rsqrt(jnp.sum(kk * kk, axis=0, keepdims=True) + 1e-6)
    k = k0 * (1.0 + (a - 1.0) * ka_ref[...])
    a_vec = -kk
    b_vec = kk * a
    w = jnp.exp(lw_ref[...])

    def body(i, carry):
        s0 = s0_ref[i]
        sa = jnp.sum(s0 * a_vec, axis=0, keepdims=True)
        s = s0 * w + sa * b_vec + v_ref[pl.ds(i, 1), :] * k
        s_ref[i] = s
        y_ref[pl.ds(i, 1), :] = jnp.sum(s * r, axis=0, keepdims=True)
        return carry

    lax.fori_loop(0, n, body, 0, unroll=8)
    y = y_ref[...]
    mean = jnp.mean(y, axis=0, keepdims=True)
    var = jnp.mean(jnp.square(y - mean), axis=0, keepdims=True)
    y = (y - mean) * lax.rsqrt(var + RWKV_GN_EPS) * lnw_ref[...] + lnb_ref[...]
    y = y + jnp.sum(r * k * rk_ref[...], axis=0, keepdims=True) * v
    o_ref[...] = y * gate_ref[...]


def rwkv_step(r, lw, k, v, a, gate, kk_w, ka_w, rk, lnw, lnb, s0_t):
    d, db = r.shape
    n = RWKV_HEAD_DIM
    n_heads = d // n
    tok = pl.BlockSpec((n, db), lambda h: (h, 0))
    par = pl.BlockSpec((n, 1), lambda h: (h, 0))
    st = pl.BlockSpec((None, n, n, db), lambda h: (h, 0, 0, 0))
    return pl.pallas_call(
        _rwkv_step_kernel,
        grid=(n_heads,),
        in_specs=[tok] * 6 + [par] * 5 + [st],
        out_specs=[tok, st],
        out_shape=[jax.ShapeDtypeStruct((d, db), F32), jax.ShapeDtypeStruct(s0_t.shape, F32)],
        scratch_shapes=[pltpu.VMEM((n, db), F32)],
        compiler_params=_params("parallel"),
        name="rwkv_step",
    )(r, lw, k, v, a, gate, kk_w, ka_w, rk, lnw, lnb, s0_t)


def _head_rms(x, gain, n_heads):
    m = x.shape[0]
    xh = x.reshape(m, n_heads, -1)
    return (xh * lax.rsqrt(jnp.mean(xh * xh, axis=-1, keepdims=True) + NORM_EPS) * gain).reshape(m, -1)


def _fox_project(x, gain, w_in, b_f, q_gain, k_gain):
    width = (w_in.shape[1] // LANES) * LANES
    n_heads = w_in.shape[1] - width
    w = _bf(jnp.concatenate([w_in[:, :width], _pad_cols(w_in[:, width:], LANES)], axis=1))
    h = mm(x, w, gain=gain)
    d = width // 4
    q = _head_rms(h[:, :d], q_gain, n_heads) * (FOX_HEAD_DIM ** -0.5)
    k = _head_rms(h[:, d:2 * d], k_gain, n_heads)
    v = h[:, 2 * d:3 * d]
    gate = _sigmoid(h[:, 3 * d:4 * d])
    log_f = jax.nn.log_sigmoid(h[:, width:width + n_heads] + b_f)
    return q, k, v, gate, log_f


def _fox_augment(q, k, v, cum, batch, seq, n_heads):
    m = batch * seq
    dh = FOX_HEAD_DIM
    c3 = jnp.stack(_split3(cum.reshape(m, n_heads)), axis=-1)
    ones3 = jnp.ones((m, n_heads, 3), BF16)
    pad = jnp.zeros((m, n_heads, LANES - dh - 6), BF16)
    qa = jnp.concatenate([_bf(q).reshape(m, n_heads, dh), c3, ones3, pad], axis=-1)
    qt = qa.reshape(batch, seq, n_heads // 2, 2, LANES).transpose(0, 2, 3, 4, 1)
    ka = jnp.concatenate([_bf(k).reshape(m, n_heads, dh), ones3, -c3, pad], axis=-1).reshape(m, n_heads * LANES)
    tk = min(seq, FOX_K_TILE)
    va = jnp.concatenate([_bf(v).reshape(m, n_heads, dh), jnp.ones((m, n_heads, 1), BF16),
                          jnp.zeros((m, n_heads, LANES - dh - 1), BF16)], axis=-1)
    vt = va.reshape(batch, seq // tk, tk, n_heads // 2, 2, LANES).transpose(0, 3, 1, 4, 5, 2)
    return qt, ka, vt


def fox_layer(hp, hs, gain, w_in, b_f, q_gain, k_gain, w_out, layer_idx, cache_k_t, cache_v_t, cache_logf_t,
              page_table, batch, seq):
    n_heads = b_f.shape[0]
    dh = FOX_HEAD_DIM
    w_out_b = _bf(w_out)
    q, k, v, gate, lf = _fox_project(hp, gain, w_in, b_f, q_gain, k_gain)
    cum = jnp.cumsum(lf.reshape(batch, seq, n_heads), axis=1)
    qt, ka, vt = _fox_augment(q, k, v, cum, batch, seq, n_heads)
    og = fox_flash(qt, ka, vt, gate, batch=batch, seq=seq)
    hp = mm(og, w_out_b, res=hp)
    outs_p = (k.reshape(batch, seq, n_heads, dh), v.reshape(batch, seq, n_heads, dh), lf.reshape(batch, seq, n_heads))
    db = hs.shape[0]
    q, k, v, gate, lf = _fox_project(hs, gain, w_in, b_f, q_gain, k_gain)
    og = fox_decode(layer_idx, page_table, q.reshape(db, n_heads, dh), k.reshape(db, n_heads, dh),
                    v.reshape(db, n_heads, dh), lf.reshape(db, n_heads, 1), gate.reshape(db, 1, n_heads * dh),
                    cache_logf_t, cache_k_t, cache_v_t)
    hs = mm(og.reshape(db, n_heads * dh), w_out_b, res=hs)
    outs_s = (k.reshape(db, 1, n_heads, dh), v.reshape(db, 1, n_heads, dh), lf.reshape(db, 1, n_heads))
    return hp, hs, outs_p, outs_s


def _gdn_project(x, gain, w_in, n_heads):
    total = w_in.shape[1]
    main = total - 2 * n_heads
    w = _bf(jnp.concatenate([w_in[:, :main], _pad_cols(w_in[:, main:], LANES)], axis=1))
    h = mm(x, w, gain=gain)
    conv_ch = main - n_heads * GDN_HEAD_DIM
    return h[:, :conv_ch], h[:, conv_ch:main], h[:, main:main + n_heads], h[:, main + n_heads:main + 2 * n_heads]


def _gdn_post_conv(conv, a, beta_logit, a_log, dt_bias, n_heads):
    m = conv.shape[0]
    conv = conv * _sigmoid(conv)
    qk_w = (n_heads // 2) * GDN_HEAD_DIM

    def l2(x):
        xh = x.reshape(m, n_heads // 2, GDN_HEAD_DIM)
        return (xh * lax.rsqrt(jnp.sum(xh * xh, axis=-1, keepdims=True) + 1e-6)).reshape(m, qk_w)

    q = l2(conv[:, :qk_w]) * (GDN_HEAD_DIM ** -0.5)
    k = l2(conv[:, qk_w:2 * qk_w])
    v = conv[:, 2 * qk_w:]
    beta = _sigmoid(beta_logit)
    g = -jnp.exp(a_log) * jax.nn.softplus(a + dt_bias)
    return q, k, v, beta, g


def gdn_layer(hp, hs, gain, w_in, conv_w, a_log, dt_bias, out_gain, w_out, conv_state, s_state, batch, seq):
    n_heads = a_log.shape[0]
    w_out_b = _bf(w_out)
    qkv, z, a, bl = _gdn_project(hp, gain, w_in, n_heads)
    ch = qkv.shape[1]
    xc = jnp.concatenate([jnp.zeros((batch, GDN_CONV_W - 1, ch), F32), qkv.reshape(batch, seq, ch)], axis=1)
    conv = xc[:, 0:seq] * conv_w[0]
    for i in range(1, GDN_CONV_W):
        conv = conv + xc[:, i:i + seq] * conv_w[i]
    q, k, v, beta, g = _gdn_post_conv(conv.reshape(batch * seq, ch), a, bl, a_log, dt_bias, n_heads)
    og, s_p = gdn_chunk_scan(q, k, v, z, beta, g, out_gain, batch=batch, seq=seq)
    hp = mm(og, w_out_b, res=hp)
    conv_p = xc[:, seq:]
    db = hs.shape[0]
    qkv, z, a, bl = _gdn_project(hs, gain, w_in, n_heads)
    xc = jnp.concatenate([conv_state, qkv[:, None, :]], axis=1)
    conv = xc[:, 0] * conv_w[0]
    for i in range(1, GDN_CONV_W):
        conv = conv + xc[:, i] * conv_w[i]
    q, k, v, beta, g = _gdn_post_conv(conv, a, bl, a_log, dt_bias, n_heads)
    og, s_s = gdn_step(q[:, None], k[:, None], v[:, None], z[:, None], beta[:, None], g[:, None], out_gain, s_state)
    hs = mm(og.reshape(db, -1), w_out_b, res=hs)
    conv_s = xc[:, 1:]
    return hp, hs, (s_p, conv_p), (s_s, conv_s)


def _rwkv_project(u, u_prev, mu, w0, w1, w2, a0, a1, a2, g1, g2, w_r, w_k, w_v):
    du = u_prev - u
    xr, xw, xk, xv, xa, xg = (u + du * mu[i] for i in range(6))
    lora = lambda w: -(-w.shape[1] // LANES) * LANES
    r = mm(xr, _bf(w_r))
    k = mm(xk, _bf(w_k))
    v = mm(xv, _bf(w_v))
    nw, na, ng = lora(w1), lora(a1), lora(g1)
    w_l = mm(jnp.tanh(mm(xw, _bf(_pad_cols(w1, nw)))), _bf(_pad_rows(w2, nw)))
    a_l = mm(mm(xa, _bf(_pad_cols(a1, na))), _bf(_pad_rows(a2, na)))
    gate = mm(_sigmoid(mm(xg, _bf(_pad_cols(g1, ng)))), _bf(_pad_rows(g2, ng)))
    w_raw = -jax.nn.softplus(-(w0 + w_l)) - 0.5
    log_decay = -jnp.exp(w_raw)
    a = _sigmoid(a0 + a_l)
    return r, log_decay, k, v, a, gate


def rwkv_layer(hp, hs, gain, mu, w0, w1, w2, a0, a1, a2, g1, g2, k_k, k_a, r_k, w_r, w_k, w_v, w_o, ln_w, ln_b,
               shift_state, wkv_state, batch, seq):
    d = hp.shape[1]
    w_o_b = _bf(w_o)
    proj_w = (mu, w0, w1, w2, a0, a1, a2, g1, g2, w_r, w_k, w_v)
    chan = (k_k, k_a, r_k.reshape(d), ln_w, ln_b)
    u = _rms(hp, gain).reshape(batch, seq, d)
    u_prev = jnp.concatenate([jnp.zeros((batch, 1, d), F32), u[:, :-1]], axis=1)
    toks = _rwkv_project(u.reshape(batch * seq, d), u_prev.reshape(batch * seq, d), *proj_w)
    og, s_p = rwkv_chunk_scan(*toks, *(p.reshape(1, d) for p in chan), batch=batch, seq=seq)
    hp = mm(og, w_o_b, res=hp)
    shift_p = u[:, -1]
    us = _rms(hs, gain)
    toks = _rwkv_project(us, shift_state, *proj_w)
    og_t, s_t = rwkv_step(*(t.T for t in toks), *(p.reshape(d, 1) for p in chan), wkv_state.transpose(1, 2, 3, 0))
    hs = mm(og_t.T, w_o_b, res=hs)
    return hp, hs, (shift_p, s_p), (us, s_t.transpose(3, 0, 1, 2))


def kernel(x_prompt, x_sample, cache_k, cache_v, cache_logf, page_table, state_gdn_s, state_gdn_conv, state_rwkv_shift, state_rwkv_wkv, norm_mix, norm_ffn, norm_final, fox_w_in, fox_b_f, fox_q_norm, fox_k_norm, fox_w_out, gdn_w_in, gdn_conv_w, gdn_a_log, gdn_dt_bias, gdn_out_norm, gdn_w_out, rwkv_mu, rwkv_w0, rwkv_w1, rwkv_w2, rwkv_a0, rwkv_a1, rwkv_a2, rwkv_g1, rwkv_g2, rwkv_k_k, rwkv_k_a, rwkv_r_k, rwkv_w_r, rwkv_w_k, rwkv_w_v, rwkv_w_o, rwkv_ln_w, rwkv_ln_b, ffn_w_in, ffn_w_out):
    batch, seq, d = x_prompt.shape
    db = x_sample.shape[0]
    depth = norm_mix.shape[0]
    hp = x_prompt.reshape(batch * seq, d)
    hs = x_sample.reshape(db, d)
    cache_k_t = cache_k.transpose(0, 1, 3, 4, 2)
    cache_v_t = cache_v.transpose(0, 1, 3, 4, 2)
    cache_logf_t = cache_logf.transpose(0, 1, 3, 2)
    fox_p, fox_s, gdn_p, gdn_s, rwkv_p, rwkv_s = [], [], [], [], [], []
    for layer in range(depth):
        kind, j = layer % 3, layer // 3
        if kind == 0:
            hp, hs, op, os_ = fox_layer(hp, hs, norm_mix[layer], fox_w_in[j], fox_b_f[j], fox_q_norm[j], fox_k_norm[j],
                                        fox_w_out[j], j, cache_k_t, cache_v_t, cache_logf_t, page_table, batch, seq)
            fox_p.append(op)
            fox_s.append(os_)
        elif kind == 1:
            hp, hs, op, os_ = gdn_layer(hp, hs, norm_mix[layer], gdn_w_in[j], gdn_conv_w[j], gdn_a_log[j],
                                        gdn_dt_bias[j], gdn_out_norm[j], gdn_w_out[j], state_gdn_conv[j],
                                        state_gdn_s[j], batch, seq)
            gdn_p.append(op)
            gdn_s.append(os_)
        else:
            hp, hs, op, os_ = rwkv_layer(hp, hs, norm_mix[layer], rwkv_mu[j], rwkv_w0[j], rwkv_w1[j], rwkv_w2[j],
                                         rwkv_a0[j], rwkv_a1[j], rwkv_a2[j], rwkv_g1[j], rwkv_g2[j], rwkv_k_k[j],
                                         rwkv_k_a[j], rwkv_r_k[j], rwkv_w_r[j], rwkv_w_k[j], rwkv_w_v[j],
                                         rwkv_w_o[j], rwkv_ln_w[j], rwkv_ln_b[j], state_rwkv_shift[j],
                                         state_rwkv_wkv[j], batch, seq)
            rwkv_p.append(op)
            rwkv_s.append(os_)
        final = norm_final if layer == depth - 1 else None
        w_in_b, w_out_b = _bf(ffn_w_in[layer]), _bf(ffn_w_out[layer])
        hp = ffn(hp, norm_ffn[layer], w_in_b, w_out_b, final)
        hs = ffn(hs, norm_ffn[layer], w_in_b, w_out_b, final)
    stack = lambda items, i: jnp.stack([it[i] for it in items])
    return (hp.reshape(batch, seq, d), hs.reshape(db, 1, d),
            stack(fox_p, 0), stack(fox_p, 1), stack(fox_p, 2),
            stack(fox_s, 0), stack(fox_s, 1), stack(fox_s, 2),
            stack(gdn_p, 0), stack(gdn_p, 1), stack(gdn_s, 0), stack(gdn_s, 1),
            stack(rwkv_p, 0), stack(rwkv_p, 1), stack(rwkv_s, 0), stack(rwkv_s, 1))
```

```python
import functools

import jax
import jax.numpy as jnp
from jax import lax
from jax.experimental import pallas as pl
from jax.experimental.pallas import tpu as pltpu

F32 = jnp.float32
BF16 = jnp.bfloat16
HIGHEST = lax.Precision.HIGHEST

NORM_EPS = 1e-6
RWKV_GN_EPS = 64e-5
LANES = 128
VMEM_LIMIT_BYTES = 48 * 1024 * 1024
NEG_BIG = -1e30

FOX_HEAD_DIM = 64
GDN_HEAD_DIM = 128
GDN_CONV_W = 4
RWKV_HEAD_DIM = 64
CHUNK = 64
FOX_Q_TILE = 256
FOX_K_TILE = 512
DECODE_PAGES_PER_STEP = 4


def _params(*semantics):
    return pltpu.CompilerParams(dimension_semantics=semantics, vmem_limit_bytes=VMEM_LIMIT_BYTES)


def _sigmoid(x):
    return 1.0 / (1.0 + jnp.exp(-x))


def _rms(x, gain):
    return x * lax.rsqrt(jnp.mean(x * x, axis=-1, keepdims=True) + NORM_EPS) * gain


def _dot(a, b):
    return jnp.dot(a, b, preferred_element_type=F32)


def _dot_nt(a, b):
    return lax.dot_general(a, b, (((1,), (1,)), ((), ())), preferred_element_type=F32)


def _dot_tn(a, b):
    return lax.dot_general(a, b, (((0,), (0,)), ((), ())), preferred_element_type=F32)


def _dot_hi(a, b):
    return jnp.dot(a, b, preferred_element_type=F32, precision=HIGHEST)


def _dot_nt_hi(a, b):
    return lax.dot_general(a, b, (((1,), (1,)), ((), ())), preferred_element_type=F32, precision=HIGHEST)


def _dot_tn_hi(a, b):
    return lax.dot_general(a, b, (((0,), (0,)), ((), ())), preferred_element_type=F32, precision=HIGHEST)


def _bf(x):
    return x.astype(BF16)


def _top_bits(x):
    bits = lax.bitcast_convert_type(x, jnp.int32) & jnp.int32(-65536)
    return lax.bitcast_convert_type(bits, F32)


def _split2(x):
    hi = _top_bits(x)
    return _bf(hi), _bf(x - hi)


def _split3(x):
    hi = _top_bits(x)
    r = x - hi
    mid = _top_bits(r)
    return _bf(hi), _bf(mid), _bf(r - mid)


def _dot_x3(a2, b2):
    (ah, al), (bh, bl) = a2, b2
    return _dot(ah, bh) + (_dot(ah, bl) + _dot(al, bh))


def _pad_cols(w, n):
    return jnp.pad(w, ((0, 0), (0, n - w.shape[1])))


def _pad_rows(w, n):
    return jnp.pad(w, ((0, n - w.shape[0]), (0, 0)))


def _col_tile(n, cap=1536):
    best = LANES
    for t in range(LANES, min(n, cap) + 1, LANES):
        if n % t == 0:
            best = t
    return best


def _mm_kernel(*refs, norm, residual):
    it = iter(refs)
    x_ref = next(it)
    g_ref = next(it) if norm else None
    w_ref = next(it)
    r_ref = next(it) if residual else None
    o_ref = next(it)
    xn_ref = next(it) if norm else None
    if norm:
        @pl.when(pl.program_id(1) == 0)
        def _():
            xn_ref[...] = _bf(_rms(x_ref[...], g_ref[...]))
        a = xn_ref[...]
    else:
        a = _bf(x_ref[...])
    acc = _dot(a, w_ref[...])
    if residual:
        acc = acc + r_ref[...]
    o_ref[...] = acc.astype(o_ref.dtype)


def mm(x, w, *, gain=None, res=None, out_dtype=F32):
    m, k = x.shape
    n = w.shape[1]
    tm = min(m, 512)
    tn = _col_tile(n)
    norm, residual = gain is not None, res is not None
    in_specs = [pl.BlockSpec((tm, k), lambda i, j: (i, 0))]
    args = [x]
    if norm:
        in_specs.append(pl.BlockSpec((1, k), lambda i, j: (0, 0)))
        args.append(gain.reshape(1, k))
    in_specs.append(pl.BlockSpec((k, tn), lambda i, j: (0, j)))
    args.append(w)
    if residual:
        in_specs.append(pl.BlockSpec((tm, tn), lambda i, j: (i, j)))
        args.append(res)
    return pl.pallas_call(
        functools.partial(_mm_kernel, norm=norm, residual=residual),
        grid=(m // tm, n // tn),
        in_specs=in_specs,
        out_specs=pl.BlockSpec((tm, tn), lambda i, j: (i, j)),
        out_shape=jax.ShapeDtypeStruct((m, n), out_dtype),
        scratch_shapes=[pltpu.VMEM((tm, k), BF16)] if norm else [],
        compiler_params=_params("parallel", "arbitrary"),
        name="dense",
    )(*args)


def _ffn_kernel(*refs, final):
    if final:
        x_ref, g_ref, wg_ref, wu_ref, wo_ref, fg_ref, o_ref, xn_ref, acc_ref = refs
    else:
        x_ref, g_ref, wg_ref, wu_ref, wo_ref, o_ref, xn_ref, acc_ref = refs
    k = pl.program_id(1)

    @pl.when(k == 0)
    def _():
        xn_ref[...] = _bf(_rms(x_ref[...], g_ref[...]))
        acc_ref[...] = jnp.zeros_like(acc_ref)

    xn = xn_ref[...]
    gate = _dot(xn, wg_ref[...])
    up = _dot(xn, wu_ref[...])
    act = _bf(gate * _sigmoid(gate) * up)
    acc_ref[...] += _dot(act, wo_ref[...])

    @pl.when(k == pl.num_programs(1) - 1)
    def _():
        out = x_ref[...] + acc_ref[...]
        if final:
            out = _rms(out, fg_ref[...])
        o_ref[...] = out


def ffn(x, gain, w_in, w_out, final_gain=None):
    m, d = x.shape
    hidden = w_out.shape[0]
    th = 256
    nk = hidden // th
    tm = min(m, 512)
    final = final_gain is not None
    in_specs = [
        pl.BlockSpec((tm, d), lambda i, k: (i, 0)),
        pl.BlockSpec((1, d), lambda i, k: (0, 0)),
        pl.BlockSpec((d, th), lambda i, k: (0, k)),
        pl.BlockSpec((d, th), lambda i, k: (0, k + nk)),
        pl.BlockSpec((th, d), lambda i, k: (k, 0)),
    ]
    args = [x, gain.reshape(1, d), w_in, w_in, w_out]
    if final:
        in_specs.append(pl.BlockSpec((1, d), lambda i, k: (0, 0)))
        args.append(final_gain.reshape(1, d))
    return pl.pallas_call(
        functools.partial(_ffn_kernel, final=final),
        grid=(m // tm, nk),
        in_specs=in_specs,
        out_specs=pl.BlockSpec((tm, d), lambda i, k: (i, 0)),
        out_shape=jax.ShapeDtypeStruct((m, d), F32),
        scratch_shapes=[pltpu.VMEM((tm, d), BF16), pltpu.VMEM((tm, d), F32)],
        compiler_params=_params("parallel", "arbitrary"),
        name="swiglu",
    )(*args)


def _fox_flash_kernel(q_ref, k_ref, vt_ref, g_ref, o_ref, m_ref, acc_ref, s_buf, p_buf, a_buf, *, tq, tk):
    i = pl.program_id(2)
    dh = FOX_HEAD_DIM
    heads = range(2)
    n_full = (i * tq) // tk
    m_ref[...] = jnp.full_like(m_ref, NEG_BIG)
    acc_ref[...] = jnp.zeros_like(acc_ref)
    p_buf[...] = jnp.zeros_like(p_buf)
    a_buf[...] = jnp.ones_like(a_buf)
    key = lax.broadcasted_iota(jnp.int32, (tk, tq), 0)
    qry = lax.broadcasted_iota(jnp.int32, (tk, tq), 1)

    def scores(j, slot):
        off = pl.multiple_of(j * tk, tk)
        for hh in heads:
            s_buf[slot, hh] = _dot(k_ref[pl.ds(off, tk), hh * LANES:(hh + 1) * LANES], q_ref[hh])

    def softmax(j, slot, masked):
        ss = [s_buf[slot, hh] for hh in heads]
        if masked:
            ss = [jnp.where(key + j * tk <= qry + i * tq, s, NEG_BIG) for s in ss]
        m_prev = [m_ref[hh] for hh in heads]
        m_new = [jnp.maximum(m_prev[hh], jnp.max(ss[hh], axis=0, keepdims=True)) for hh in heads]
        for hh in heads:
            p_buf[slot, hh] = _bf(jnp.exp(ss[hh] - m_new[hh]))
            a_buf[slot, hh] = jnp.exp(m_prev[hh] - m_new[hh])
            m_ref[hh] = m_new[hh]

    def values(j, slot):
        jj = jnp.maximum(j, 0)
        for hh in heads:
            acc_ref[hh] = a_buf[slot, hh] * acc_ref[hh] + _dot(vt_ref[jj, hh], p_buf[slot, hh])

    scores(0, 0)

    def pair(u, carry):
        j = 2 * u
        scores(j + 1, 1)
        softmax(j, 0, False)
        values(j - 1, 1)
        scores(j + 2, 0)
        softmax(j + 1, 1, False)
        values(j, 0)
        return carry

    n_pairs = n_full // 2
    lax.fori_loop(0, n_pairs, pair, 0)
    j = 2 * n_pairs

    @pl.when(n_full - j == 1)
    def _():
        scores(j + 1, 1)
        softmax(j, 0, False)
        values(j - 1, 1)
        softmax(j + 1, 1, True)
        values(j, 0)
        values(j + 1, 1)

    @pl.when(n_full - j == 0)
    def _():
        softmax(j, 0, True)
        values(j - 1, 1)
        values(j, 0)

    halves = []
    for hh in range(2):
        a = acc_ref[hh].T
        halves.append(a[:, :dh] / a[:, dh:dh + 1])
    o_ref[...] = _bf(jnp.concatenate(halves, axis=-1) * g_ref[...])


def fox_flash(qt, ka, vt, gate, *, batch, seq):
    m = ka.shape[0]
    n_pairs, nk, tk = vt.shape[1], vt.shape[2], vt.shape[5]
    tq = min(seq, FOX_Q_TILE)
    nq = seq // tq
    return pl.pallas_call(
        functools.partial(_fox_flash_kernel, tq=tq, tk=tk),
        grid=(batch, n_pairs, nq),
        in_specs=[
            pl.BlockSpec((None, None, 2, LANES, tq), lambda b, h, i: (b, h, 0, 0, i)),
            pl.BlockSpec((seq, 2 * LANES), lambda b, h, i: (b, h)),
            pl.BlockSpec((None, None, nk, 2, LANES, tk), lambda b, h, i: (b, h, 0, 0, 0, 0)),
            pl.BlockSpec((tq, LANES), lambda b, h, i: (b * nq + i, h)),
        ],
        out_specs=pl.BlockSpec((tq, LANES), lambda b, h, i: (b * nq + i, h)),
        out_shape=jax.ShapeDtypeStruct((m, n_pairs * LANES), BF16),
        scratch_shapes=[pltpu.VMEM((2, 1, tq), F32), pltpu.VMEM((2, LANES, tq), F32),
                        pltpu.VMEM((2, 2, tk, tq), F32), pltpu.VMEM((2, 2, tk, tq), BF16),
                        pltpu.VMEM((2, 2, 1, tq), F32)],
        compiler_params=_params("parallel", "parallel", "arbitrary"),
        name="fox_flash",
    )(qt, ka, vt, gate)


def _fox_decode_kernel(pt_ref, q_ref, kn_ref, vn_ref, lfn_ref, g_ref, *refs, pps):
    lf_refs, k_refs, v_refs = refs[:pps], refs[pps:2 * pps], refs[2 * pps:3 * pps]
    o_ref, qb_ref, m_ref, l_ref, carry_ref, acc_ref = refs[3 * pps:]
    p = pl.program_id(1)
    n_heads, dh, page = k_refs[0].shape
    heads = range(n_heads)

    @pl.when(p == 0)
    def _():
        eye = (lax.broadcasted_iota(jnp.int32, (dh, dh), 0) == lax.broadcasted_iota(jnp.int32, (dh, dh), 1)).astype(F32)
        q_t = _dot_nt_hi(eye, q_ref[...])
        kn_t = _dot_nt_hi(eye, kn_ref[...])
        vn_t = _dot_nt_hi(eye, vn_ref[...])
        lane0 = lax.broadcasted_iota(jnp.int32, (dh, page), 1) == 0
        for h in heads:
            q_col = q_t[:, h:h + 1]
            qb_ref[h] = jnp.broadcast_to(q_col, (dh, page))
            s_new = jnp.sum(q_col * kn_t[:, h:h + 1], axis=0, keepdims=True)
            m_ref[h:h + 1, :] = jnp.broadcast_to(s_new, (1, page))
            acc_ref[h] = jnp.where(lane0, jnp.broadcast_to(vn_t[:, h:h + 1], (dh, page)), 0.0)
        l_ref[...] = jnp.ones_like(l_ref)
        carry_ref[...] = jnp.broadcast_to(lfn_ref[...], (n_heads, page))

    later = (lax.broadcasted_iota(jnp.int32, (page, page), 0) > lax.broadcasted_iota(jnp.int32, (page, page), 1)).astype(F32)
    carry = carry_ref[...]
    ss = []
    for i in range(pps):
        lf = lf_refs[i][...]
        bias = carry + _dot_hi(lf, later)
        carry = carry + jnp.sum(lf, axis=-1, keepdims=True)
        ss.append(jnp.concatenate([jnp.sum(qb_ref[h] * k_refs[i][h], axis=0, keepdims=True) for h in heads], axis=0)
                  + bias)
    carry_ref[...] = carry
    m_prev = m_ref[...]
    m_new = m_prev
    for s in ss:
        m_new = jnp.maximum(m_new, jnp.max(s, axis=-1, keepdims=True))
    alpha = jnp.exp(m_prev - m_new)
    prs = [jnp.exp(s - m_new) for s in ss]
    l_new = alpha * l_ref[...]
    for pr in prs:
        l_new = l_new + jnp.sum(pr, axis=-1, keepdims=True)
    l_ref[...] = l_new
    m_ref[...] = m_new
    for h in heads:
        acc = alpha[h:h + 1, :] * acc_ref[h]
        for i in range(pps):
            acc = acc + prs[i][h:h + 1, :] * v_refs[i][h]
        acc_ref[h] = acc

    @pl.when(p == pl.num_programs(1) - 1)
    def _():
        for h in heads:
            acc_ref[h] = acc_ref[h] / l_ref[h:h + 1, :]
        ones = jnp.ones((8, page), F32)
        o = _dot_nt_hi(ones, acc_ref[...].reshape(n_heads * dh, page))
        o_ref[...] = o[0:1, :] * g_ref[...]


def fox_decode(layer_idx, page_table, q, k_new, v_new, lf_new, gate, cache_logf_t, cache_k_t, cache_v_t):
    db, n_heads, dh = q.shape
    n_pages = page_table.shape[1]
    page = cache_k_t.shape[4]
    pps = DECODE_PAGES_PER_STEP if n_pages % DECODE_PAGES_PER_STEP == 0 else 1
    vec = pl.BlockSpec((None, n_heads, dh), lambda b, p, pt: (b, 0, 0))
    row = pl.BlockSpec((None, 1, n_heads * dh), lambda b, p, pt: (b, 0, 0))

    def past(i):
        return lambda b, p, pt: pt[b, n_pages - 1 - (p * pps + i)]

    lf_specs = [pl.BlockSpec((None, None, n_heads, page), lambda b, p, pt, f=past(i): (layer_idx, f(b, p, pt), 0, 0))
                for i in range(pps)]
    kv_specs = [pl.BlockSpec((None, None, n_heads, dh, page),
                             lambda b, p, pt, f=past(i): (layer_idx, f(b, p, pt), 0, 0, 0)) for i in range(pps)]
    return pl.pallas_call(
        functools.partial(_fox_decode_kernel, pps=pps),
        grid_spec=pltpu.PrefetchScalarGridSpec(
            num_scalar_prefetch=1,
            grid=(db, n_pages // pps),
            in_specs=[vec, vec, vec, pl.BlockSpec((None, n_heads, 1), lambda b, p, pt: (b, 0, 0)), row]
            + lf_specs + kv_specs + kv_specs,
            out_specs=row,
            scratch_shapes=[pltpu.VMEM((n_heads, dh, page), F32), pltpu.VMEM((n_heads, page), F32),
                            pltpu.VMEM((n_heads, page), F32), pltpu.VMEM((n_heads, page), F32),
                            pltpu.VMEM((n_heads, dh, page), F32)],
        ),
        out_shape=jax.ShapeDtypeStruct((db, 1, n_heads * dh), F32),
        compiler_params=_params("parallel", "arbitrary"),
        name="fox_decode",
    )(page_table, q, k_new, v_new, lf_new, gate, *([cache_logf_t] * pps), *([cache_k_t] * pps), *([cache_v_t] * pps))


def _tri_masks(c):
    row = lax.broadcasted_iota(jnp.int32, (c, c), 0)
    col = lax.broadcasted_iota(jnp.int32, (c, c), 1)
    return row >= col, row > col, row == col


def _cumsum_rows(tri_b, x):
    hi, mid, lo = _split3(x)
    return _dot(tri_b, hi) + (_dot(tri_b, mid) + _dot(tri_b, lo))


def _neumann_inverses(mats, eye, c):
    steps = max(c.bit_length() - 2, 0)
    ps = [eye + a for a in mats]
    if steps == 0:
        return ps
    splits = [_split2(a) for a in mats]
    aks = [_dot_x3(s, s) for s in splits]
    for step in range(steps):
        last = step == steps - 1
        nxt = []
        for p, ak in zip(ps, aks):
            ak2 = _split2(ak)
            lhs = p if last else jnp.concatenate([p, ak], axis=0)
            nxt.append(_dot_x3(_split2(lhs), ak2))
        ps = [p + n[:c] for p, n in zip(ps, nxt)]
        aks = [None if last else n[c:] for n in nxt]
    return ps


def _gdn_chunk_kernel(q_ref, k_ref, v_ref, z_ref, bcol_ref, gcol_ref, grow_ref, gain_ref, o_ref, s_out_ref, s_ref, *, c, hb):
    ch = pl.program_id(2)
    dk = GDN_HEAD_DIM

    @pl.when(ch == 0)
    def _():
        s_ref[...] = jnp.zeros_like(s_ref)

    incl, strict, diag = _tri_masks(c)
    eye = jnp.where(diag, 1.0, 0.0).astype(F32)
    tri_b = jnp.where(incl, 1.0, 0.0).astype(BF16)
    gc_col = _cumsum_rows(tri_b, gcol_ref[...])
    g3 = _split3(grow_ref[...])
    gc_row = _dot_nt(g3[0], tri_b) + (_dot_nt(g3[1], tri_b) + _dot_nt(g3[2], tri_b))
    heads = range(hb)
    vs = [slice(hh * dk, (hh + 1) * dk) for hh in heads]
    q = [q_ref[:, (hh // 2) * dk:(hh // 2 + 1) * dk] for hh in heads]
    k = [k_ref[:, (hh // 2) * dk:(hh // 2 + 1) * dk] for hh in heads]
    beta = [bcol_ref[:, hh:hh + 1] for hh in heads]
    gcc = [gc_col[:, hh:hh + 1] for hh in heads]
    g_last = [g[c - 1:c, :] for g in gcc]
    decay = [jnp.where(incl, jnp.exp(jnp.where(incl, gcc[hh] - gc_row[hh:hh + 1, :], 0.0)), 0.0) for hh in heads]
    kb = [k[hh] * beta[hh] for hh in heads]
    kbf = [_bf(x) for x in k]
    kk = [_dot_nt(_bf(kb[hh]), kbf[hh]) for hh in heads]
    qk = [_dot_nt(_bf(q[hh]), kbf[hh]) for hh in heads]
    neg_lower = [jnp.where(strict, -(kk[hh] * decay[hh]), 0.0) for hh in heads]
    attn = [_bf(jnp.where(incl, qk[hh] * decay[hh], 0.0)) for hh in heads]
    e_gc = [jnp.exp(g) for g in gcc]
    rhs = [_split2(jnp.concatenate([v_ref[:, vs[hh]] * beta[hh], kb[hh] * e_gc[hh]], axis=-1)) for hh in heads]
    q_dec = [_bf(q[hh] * e_gc[hh]) for hh in heads]
    k_dec = [_bf(k[hh] * jnp.exp(g_last[hh] - gcc[hh])) for hh in heads]
    t_inv = _neumann_inverses(neg_lower, eye, c)
    sol = [_dot_x3(_split2(t_inv[hh]), rhs[hh]) for hh in heads]
    s = [s_ref[hh] for hh in heads]
    sb = [_bf(x) for x in s]
    vnb = [_bf(sol[hh][:, :dk] - _dot(_bf(sol[hh][:, dk:]), sb[hh])) for hh in heads]
    o = [_dot(q_dec[hh], sb[hh]) + _dot(attn[hh], vnb[hh]) for hh in heads]
    s_new = [s[hh] * jnp.exp(g_last[hh]) + _dot_tn(k_dec[hh], vnb[hh]) for hh in heads]
    for hh in heads:
        s_ref[hh] = s_new[hh]
        z = z_ref[:, vs[hh]]
        o_ref[:, vs[hh]] = _bf(_rms(o[hh], gain_ref[...]) * (z * _sigmoid(z)))

    @pl.when(ch == pl.num_programs(2) - 1)
    def _():
        s_out_ref[...] = s_ref[...]


def gdn_chunk_scan(q, k, v, z, beta, g, gain, *, batch, seq):
    m = q.shape[0]
    n_heads = v.shape[1] // GDN_HEAD_DIM
    c = min(CHUNK, seq)
    hb = min(8, n_heads)
    n_groups = n_heads // hb
    n = seq // c
    bcol = beta.reshape(m, n_groups, hb).transpose(1, 0, 2)
    gcol = g.reshape(m, n_groups, hb).transpose(1, 0, 2)
    grow = g.reshape(batch * n, c, n_groups, hb).transpose(0, 2, 3, 1)
    qk_w = (hb // 2) * GDN_HEAD_DIM
    v_w = hb * GDN_HEAD_DIM
    row_blk = lambda b, h, t: (b * n + t, h)
    col_spec = pl.BlockSpec((None, c, hb), lambda b, h, t: (h, b * n + t, 0))
    return pl.pallas_call(
        functools.partial(_gdn_chunk_kernel, c=c, hb=hb),
        grid=(batch, n_groups, n),
        in_specs=[
            pl.BlockSpec((c, qk_w), row_blk),
            pl.BlockSpec((c, qk_w), row_blk),
            pl.BlockSpec((c, v_w), row_blk),
            pl.BlockSpec((c, v_w), row_blk),
            col_spec, col_spec,
            pl.BlockSpec((None, None, hb, c), lambda b, h, t: (b * n + t, h, 0, 0)),
            pl.BlockSpec((1, GDN_HEAD_DIM), lambda b, h, t: (0, 0)),
        ],
        out_specs=[
            pl.BlockSpec((c, v_w), row_blk),
            pl.BlockSpec((None, hb, GDN_HEAD_DIM, GDN_HEAD_DIM), lambda b, h, t: (b, h, 0, 0)),
        ],
        out_shape=[jax.ShapeDtypeStruct((m, n_heads * GDN_HEAD_DIM), BF16),
                   jax.ShapeDtypeStruct((batch, n_heads, GDN_HEAD_DIM, GDN_HEAD_DIM), F32)],
        scratch_shapes=[pltpu.VMEM((hb, GDN_HEAD_DIM, GDN_HEAD_DIM), F32)],
        compiler_params=_params("parallel", "parallel", "arbitrary"),
        name="gdn_chunk_scan",
    )(q, k, v, z, bcol, gcol, grow, gain.reshape(1, GDN_HEAD_DIM))


def _gdn_step_kernel(q_ref, k_ref, v_ref, z_ref, beta_ref, g_ref, gain_ref, s0_ref, o_ref, s_ref):
    dk = GDN_HEAD_DIM
    n_heads = s0_ref.shape[0]
    row = lax.broadcasted_iota(jnp.int32, (8, dk), 0)
    for h in range(n_heads):
        qk = slice((h // 2) * dk, (h // 2 + 1) * dk)
        vs = slice(h * dk, (h + 1) * dk)
        q = q_ref[:, qk]
        k = k_ref[:, qk]
        v = v_ref[:, vs]
        beta = beta_ref[:, h:h + 1]
        e_g = jnp.exp(g_ref[:, h:h + 1])
        s0 = s0_ref[h]
        kb = k * beta
        lhs = jnp.where(row < 4, jnp.broadcast_to(kb * e_g, (8, dk)), jnp.broadcast_to(q * e_g, (8, dk)))
        prod = _dot_hi(lhs, s0)
        v_new = v * beta - prod[0:1, :]
        o = prod[4:5, :] + jnp.sum(q * k, axis=-1, keepdims=True) * v_new
        k8 = jnp.where(row == 0, jnp.broadcast_to(k, (8, dk)), 0.0)
        v8 = jnp.broadcast_to(v_new, (8, dk))
        s_ref[h] = s0 * e_g + _dot_tn_hi(k8, v8)
        z = z_ref[:, vs]
        o_ref[:, vs] = _rms(o, gain_ref[...]) * (z * _sigmoid(z))


def gdn_step(q, k, v, z, beta, g, gain, s0):
    db = q.shape[0]
    n_heads = s0.shape[1]
    vec = lambda w: pl.BlockSpec((None, 1, w), lambda b: (b, 0, 0))
    st = pl.BlockSpec((None, n_heads, GDN_HEAD_DIM, GDN_HEAD_DIM), lambda b: (b, 0, 0, 0))
    return pl.pallas_call(
        _gdn_step_kernel,
        grid=(db,),
        in_specs=[vec(q.shape[2]), vec(q.shape[2]), vec(v.shape[2]), vec(v.shape[2]), vec(n_heads), vec(n_heads),
                  pl.BlockSpec((1, GDN_HEAD_DIM), lambda b: (0, 0)), st],
        out_specs=[vec(v.shape[2]), st],
        out_shape=[jax.ShapeDtypeStruct(v.shape, F32), jax.ShapeDtypeStruct(s0.shape, F32)],
        compiler_params=_params("parallel"),
        name="gdn_step",
    )(q, k, v, z, beta, g, gain.reshape(1, GDN_HEAD_DIM), s0)


def _rwkv_post(y, r, k, v, gate, rk, lnw, lnb):
    mean = jnp.mean(y, axis=-1, keepdims=True)
    var = jnp.mean(jnp.square(y - mean), axis=-1, keepdims=True)
    y = (y - mean) * lax.rsqrt(var + RWKV_GN_EPS) * lnw + lnb
    y = y + jnp.sum(r * k * rk, axis=-1, keepdims=True) * v
    return y * gate


def _rwkv_keys(k, a, kk_w, ka_w):
    kk = k * kk_w
    kk = kk * lax.rsqrt(jnp.sum(kk * kk, axis=-1, keepdims=True) + 1e-6)
    k = k * (1.0 + (a - 1.0) * ka_w)
    return k, -kk, kk * a


def _rwkv_chunk_kernel(r_ref, lw_ref, k_ref, v_ref, a_ref, gate_ref, kk_ref, ka_ref, rk_ref, lnw_ref, lnb_ref,
                       o_ref, s_out_ref, s_ref, *, c, hb):
    ch = pl.program_id(2)
    n = RWKV_HEAD_DIM

    @pl.when(ch == 0)
    def _():
        s_ref[...] = jnp.zeros_like(s_ref)

    incl, strict, diag = _tri_masks(c)
    eye = jnp.where(diag, 1.0, 0.0).astype(F32)
    tri_b = jnp.where(incl, 1.0, 0.0).astype(BF16)
    gi_all = _cumsum_rows(tri_b, lw_ref[...])
    heads = range(hb)
    sl = [slice(hh * n, (hh + 1) * n) for hh in heads]
    r = [r_ref[:, s_] for s_ in sl]
    v = [v_ref[:, s_] for s_ in sl]
    vb = [_bf(x) for x in v]
    keys = [_rwkv_keys(k_ref[:, s_], a_ref[:, s_], kk_ref[:, s_], ka_ref[:, s_]) for s_ in sl]
    k = [t[0] for t in keys]
    a_vec = [t[1] for t in keys]
    b_vec = [t[2] for t in keys]
    gi = [gi_all[:, s_] for s_ in sl]
    g_last = [g[c - 1:c, :] for g in gi]
    e_neg = [jnp.exp(-g) for g in gi]
    e_rest = [jnp.exp(g_last[hh] - gi[hh]) for hh in heads]
    a_t = [a_vec[hh] * jnp.exp(gi[hh] - lw_ref[:, sl[hh]]) for hh in heads]
    r_tb = [_bf(r[hh] * jnp.exp(gi[hh])) for hh in heads]
    lhs = [jnp.concatenate([_bf(a_t[hh]), r_tb[hh]], axis=0) for hh in heads]
    rhs = [_bf(jnp.concatenate([b_vec[hh] * e_neg[hh], k[hh] * e_neg[hh]], axis=0)) for hh in heads]
    tail = [_bf(jnp.concatenate([b_vec[hh] * e_rest[hh], k[hh] * e_rest[hh]], axis=0)) for hh in heads]
    m4 = [_dot_nt(lhs[hh], rhs[hh]) for hh in heads]
    a_ab = [jnp.where(strict, m[:c, :c], 0.0) for m in m4]
    a_ak = [_bf(jnp.where(strict, m[:c, c:], 0.0)) for m in m4]
    a_rb = [_bf(jnp.where(incl, m[c:, :c], 0.0)) for m in m4]
    a_rk = [_bf(jnp.where(incl, m[c:, c:], 0.0)) for m in m4]
    akv = [_dot(a_ak[hh], vb[hh]) for hh in heads]
    t_inv = _neumann_inverses(a_ab, eye, c)
    sol = [_dot_x3(_split2(t_inv[hh]), _split2(jnp.concatenate([a_t[hh], akv[hh]], axis=-1))) for hh in heads]
    s = [s_ref[hh] for hh in heads]
    sb = [_bf(x) for x in s]
    ub = [_bf(_dot_nt(_bf(sol[hh][:, :n]), sb[hh]) + sol[hh][:, n:]) for hh in heads]
    y = [_dot_nt(r_tb[hh], sb[hh]) + _dot(a_rb[hh], ub[hh]) + _dot(a_rk[hh], vb[hh]) for hh in heads]
    s_new = [s[hh] * jnp.exp(g_last[hh]) + _dot_tn(jnp.concatenate([ub[hh], vb[hh]], axis=0), tail[hh])
             for hh in heads]
    for hh in heads:
        s_ref[hh] = s_new[hh]
        o_ref[:, sl[hh]] = _bf(_rwkv_post(y[hh], r[hh], k[hh], v[hh], gate_ref[:, sl[hh]], rk_ref[:, sl[hh]],
                                          lnw_ref[:, sl[hh]], lnb_ref[:, sl[hh]]))

    @pl.when(ch == pl.num_programs(2) - 1)
    def _():
        s_out_ref[...] = s_ref[...]


def rwkv_chunk_scan(r, lw, k, v, a, gate, kk_w, ka_w, rk, lnw, lnb, *, batch, seq):
    m, d = r.shape
    n_heads = d // RWKV_HEAD_DIM
    hb = min(8, n_heads)
    w = hb * RWKV_HEAD_DIM
    c = min(CHUNK, seq)
    n = seq // c
    tok = pl.BlockSpec((c, w), lambda b, h, t: (b * n + t, h))
    par = pl.BlockSpec((1, w), lambda b, h, t: (0, h))
    return pl.pallas_call(
        functools.partial(_rwkv_chunk_kernel, c=c, hb=hb),
        grid=(batch, n_heads // hb, n),
        in_specs=[tok] * 6 + [par] * 5,
        out_specs=[tok, pl.BlockSpec((None, hb, RWKV_HEAD_DIM, RWKV_HEAD_DIM), lambda b, h, t: (b, h, 0, 0))],
        out_shape=[jax.ShapeDtypeStruct((m, d), BF16),
                   jax.ShapeDtypeStruct((batch, n_heads, RWKV_HEAD_DIM, RWKV_HEAD_DIM), F32)],
        scratch_shapes=[pltpu.VMEM((hb, RWKV_HEAD_DIM, RWKV_HEAD_DIM), F32)],
        compiler_params=_params("parallel", "parallel", "arbitrary"),
        name="rwkv_chunk_scan",
    )(r, lw, k, v, a, gate, kk_w, ka_w, rk, lnw, lnb)


def _rwkv_step_kernel(r_ref, lw_ref, k_ref, v_ref, a_ref, gate_ref, kk_ref, ka_ref, rk_ref, lnw_ref, lnb_ref,
                      s0_ref, o_ref, s_ref, y_ref):
    n = s0_ref.shape[0]
    r = r_ref[...]
    v = v_ref[...]
    k0 = k_ref[...]
    a = a_ref[...]
    kk = k0 * kk_ref[...]
    kk = kk * lax.rsqrt(jnp.sum(kk * kk, axis=0, keepdims=True) + 1e-6)
    k = k0 * (1.0 + (a - 1.0) * ka_ref[...])
    a_vec = -kk
    b_vec = kk * a
    w = jnp.exp(lw_ref[...])

    def body(i, carry):
        s0 = s0_ref[i]
        sa = jnp.sum(s0 * a_vec, axis=0, keepdims=True)
        s = s0 * w + sa * b_vec + v_ref[pl.ds(i, 1), :] * k
        s_ref[i] = s
        y_ref[pl.ds(i, 1), :] = jnp.sum(s * r, axis=0, keepdims=True)
        return carry

    lax.fori_loop(0, n, body, 0, unroll=8)
    y = y_ref[...]
    mean = jnp.mean(y, axis=0, keepdims=True)
    var = jnp.mean(jnp.square(y - mean), axis=0, keepdims=True)
    y = (y - mean) * lax.rsqrt(var + RWKV_GN_EPS) * lnw_ref[...] + lnb_ref[...]
    y = y + jnp.sum(r * k * rk_ref[...], axis=0, keepdims=True) * v
    o_ref[...] = y * gate_ref[...]


def rwkv_step(r, lw, k, v, a, gate, kk_w, ka_w, rk, lnw, lnb, s0_t):
    d, db = r.shape
    n = RWKV_HEAD_DIM
    n_heads = d // n
    tok = pl.BlockSpec((n, db), lambda h: (h, 0))
    par = pl.BlockSpec((n, 1), lambda h: (h, 0))
    st = pl.BlockSpec((None, n, n, db), lambda h: (h, 0, 0, 0))
    return pl.pallas_call(
        _rwkv_step_kernel,
        grid=(n_heads,),
        in_specs=[tok] * 6 + [par] * 5 + [st],
        out_specs=[tok, st],
        out_shape=[jax.ShapeDtypeStruct((d, db), F32), jax.ShapeDtypeStruct(s0_t.shape, F32)],
        scratch_shapes=[pltpu.VMEM((n, db), F32)],
        compiler_params=_params("parallel"),
        name="rwkv_step",
    )(r, lw, k, v, a, gate, kk_w, ka_w, rk, lnw, lnb, s0_t)


def _head_rms(x, gain, n_heads):
    m = x.shape[0]
    xh = x.reshape(m, n_heads, -1)
    return (xh * lax.rsqrt(jnp.mean(xh * xh, axis=-1, keepdims=True) + NORM_EPS) * gain).reshape(m, -1)


def _fox_project(x, gain, w_in, b_f, q_gain, k_gain):
    width = (w_in.shape[1] // LANES) * LANES
    n_heads = w_in.shape[1] - width
    w = _bf(jnp.concatenate([w_in[:, :width], _pad_cols(w_in[:, width:], LANES)], axis=1))
    h = mm(x, w, gain=gain)
    d = width // 4
    q = _head_rms(h[:, :d], q_gain, n_heads) * (FOX_HEAD_DIM ** -0.5)
    k = _head_rms(h[:, d:2 * d], k_gain, n_heads)
    v = h[:, 2 * d:3 * d]
    gate = _sigmoid(h[:, 3 * d:4 * d])
    log_f = jax.nn.log_sigmoid(h[:, width:width + n_heads] + b_f)
    return q, k, v, gate, log_f


def _fox_augment(q, k, v, cum, batch, seq, n_heads):
    m = batch * seq
    dh = FOX_HEAD_DIM
    c3 = jnp.stack(_split3(cum.reshape(m, n_heads)), axis=-1)
    ones3 = jnp.ones((m, n_heads, 3), BF16)
    pad = jnp.zeros((m, n_heads, LANES - dh - 6), BF16)
    qa = jnp.concatenate([_bf(q).reshape(m, n_heads, dh), c3, ones3, pad], axis=-1)
    qt = qa.reshape(batch, seq, n_heads // 2, 2, LANES).transpose(0, 2, 3, 4, 1)
    ka = jnp.concatenate([_bf(k).reshape(m, n_heads, dh), ones3, -c3, pad], axis=-1).reshape(m, n_heads * LANES)
    tk = min(seq, FOX_K_TILE)
    va = jnp.concatenate([_bf(v).reshape(m, n_heads, dh), jnp.ones((m, n_heads, 1), BF16),
                          jnp.zeros((m, n_heads, LANES - dh - 1), BF16)], axis=-1)
    vt = va.reshape(batch, seq // tk, tk, n_heads // 2, 2, LANES).transpose(0, 3, 1, 4, 5, 2)
    return qt, ka, vt


def fox_layer(hp, hs, gain, w_in, b_f, q_gain, k_gain, w_out, layer_idx, cache_k_t, cache_v_t, cache_logf_t,
              page_table, batch, seq):
    n_heads = b_f.shape[0]
    dh = FOX_HEAD_DIM
    w_out_b = _bf(w_out)
    q, k, v, gate, lf = _fox_project(hp, gain, w_in, b_f, q_gain, k_gain)
    cum = jnp.cumsum(lf.reshape(batch, seq, n_heads), axis=1)
    qt, ka, vt = _fox_augment(q, k, v, cum, batch, seq, n_heads)
    og = fox_flash(qt, ka, vt, gate, batch=batch, seq=seq)
    hp = mm(og, w_out_b, res=hp)
    outs_p = (k.reshape(batch, seq, n_heads, dh), v.reshape(batch, seq, n_heads, dh), lf.reshape(batch, seq, n_heads))
    db = hs.shape[0]
    q, k, v, gate, lf = _fox_project(hs, gain, w_in, b_f, q_gain, k_gain)
    og = fox_decode(layer_idx, page_table, q.reshape(db, n_heads, dh), k.reshape(db, n_heads, dh),
                    v.reshape(db, n_heads, dh), lf.reshape(db, n_heads, 1), gate.reshape(db, 1, n_heads * dh),
                    cache_logf_t, cache_k_t, cache_v_t)
    hs = mm(og.reshape(db, n_heads * dh), w_out_b, res=hs)
    outs_s = (k.reshape(db, 1, n_heads, dh), v.reshape(db, 1, n_heads, dh), lf.reshape(db, 1, n_heads))
    return hp, hs, outs_p, outs_s


def _gdn_project(x, gain, w_in, n_heads):
    total = w_in.shape[1]
    main = total - 2 * n_heads
    w = _bf(jnp.concatenate([w_in[:, :main], _pad_cols(w_in[:, main:], LANES)], axis=1))
    h = mm(x, w, gain=gain)
    conv_ch = main - n_heads * GDN_HEAD_DIM
    return h[:, :conv_ch], h[:, conv_ch:main], h[:, main:main + n_heads], h[:, main + n_heads:main + 2 * n_heads]


def _gdn_post_conv(conv, a, beta_logit, a_log, dt_bias, n_heads):
    m = conv.shape[0]
    conv = conv * _sigmoid(conv)
    qk_w = (n_heads // 2) * GDN_HEAD_DIM

    def l2(x):
        xh = x.reshape(m, n_heads // 2, GDN_HEAD_DIM)
        return (xh * lax.rsqrt(jnp.sum(xh * xh, axis=-1, keepdims=True) + 1e-6)).reshape(m, qk_w)

    q = l2(conv[:, :qk_w]) * (GDN_HEAD_DIM ** -0.5)
    k = l2(conv[:, qk_w:2 * qk_w])
    v = conv[:, 2 * qk_w:]
    beta = _sigmoid(beta_logit)
    g = -jnp.exp(a_log) * jax.nn.softplus(a + dt_bias)
    return q, k, v, beta, g


def gdn_layer(hp, hs, gain, w_in, conv_w, a_log, dt_bias, out_gain, w_out, conv_state, s_state, batch, seq):
    n_heads = a_log.shape[0]
    w_out_b = _bf(w_out)
    qkv, z, a, bl = _gdn_project(hp, gain, w_in, n_heads)
    ch = qkv.shape[1]
    xc = jnp.concatenate([jnp.zeros((batch, GDN_CONV_W - 1, ch), F32), qkv.reshape(batch, seq, ch)], axis=1)
    conv = xc[:, 0:seq] * conv_w[0]
    for i in range(1, GDN_CONV_W):
        conv = conv + xc[:, i:i + seq] * conv_w[i]
    q, k, v, beta, g = _gdn_post_conv(conv.reshape(batch * seq, ch), a, bl, a_log, dt_bias, n_heads)
    og, s_p = gdn_chunk_scan(q, k, v, z, beta, g, out_gain, batch=batch, seq=seq)
    hp = mm(og, w_out_b, res=hp)
    conv_p = xc[:, seq:]
    db = hs.shape[0]
    qkv, z, a, bl = _gdn_project(hs, gain, w_in, n_heads)
    xc = jnp.concatenate([conv_state, qkv[:, None, :]], axis=1)
    conv = xc[:, 0] * conv_w[0]
    for i in range(1, GDN_CONV_W):
        conv = conv + xc[:, i] * conv_w[i]
    q, k, v, beta, g = _gdn_post_conv(conv, a, bl, a_log, dt_bias, n_heads)
    og, s_s = gdn_step(q[:, None], k[:, None], v[:, None], z[:, None], beta[:, None], g[:, None], out_gain, s_state)
    hs = mm(og.reshape(db, -1), w_out_b, res=hs)
    conv_s = xc[:, 1:]
    return hp, hs, (s_p, conv_p), (s_s, conv_s)


def _rwkv_project(u, u_prev, mu, w0, w1, w2, a0, a1, a2, g1, g2, w_r, w_k, w_v):
    du = u_prev - u
    xr, xw, xk, xv, xa, xg = (u + du * mu[i] for i in range(6))
    lora = lambda w: -(-w.shape[1] // LANES) * LANES
    r = mm(xr, _bf(w_r))
    k = mm(xk, _bf(w_k))
    v = mm(xv, _bf(w_v))
    nw, na, ng = lora(w1), lora(a1), lora(g1)
    w_l = mm(jnp.tanh(mm(xw, _bf(_pad_cols(w1, nw)))), _bf(_pad_rows(w2, nw)))
    a_l = mm(mm(xa, _bf(_pad_cols(a1, na))), _bf(_pad_rows(a2, na)))
    gate = mm(_sigmoid(mm(xg, _bf(_pad_cols(g1, ng)))), _bf(_pad_rows(g2, ng)))
    w_raw = -jax.nn.softplus(-(w0 + w_l)) - 0.5
    log_decay = -jnp.exp(w_raw)
    a = _sigmoid(a0 + a_l)
    return r, log_decay, k, v, a, gate


def rwkv_layer(hp, hs, gain, mu, w0, w1, w2, a0, a1, a2, g1, g2, k_k, k_a, r_k, w_r, w_k, w_v, w_o, ln_w, ln_b,
               shift_state, wkv_state, batch, seq):
    d = hp.shape[1]
    w_o_b = _bf(w_o)
    proj_w = (mu, w0, w1, w2, a0, a1, a2, g1, g2, w_r, w_k, w_v)
    chan = (k_k, k_a, r_k.reshape(d), ln_w, ln_b)
    u = _rms(hp, gain).reshape(batch, seq, d)
    u_prev = jnp.concatenate([jnp.zeros((batch, 1, d), F32), u[:, :-1]], axis=1)
    toks = _rwkv_project(u.reshape(batch * seq, d), u_prev.reshape(batch * seq, d), *proj_w)
    og, s_p = rwkv_chunk_scan(*toks, *(p.reshape(1, d) for p in chan), batch=batch, seq=seq)
    hp = mm(og, w_o_b, res=hp)
    shift_p = u[:, -1]
    us = _rms(hs, gain)
    toks = _rwkv_project(us, shift_state, *proj_w)
    og_t, s_t = rwkv_step(*(t.T for t in toks), *(p.reshape(d, 1) for p in chan), wkv_state.transpose(1, 2, 3, 0))
    hs = mm(og_t.T, w_o_b, res=hs)
    return hp, hs, (shift_p, s_p), (us, s_t.transpose(3, 0, 1, 2))


def kernel(x_prompt, x_sample, cache_k, cache_v, cache_logf, page_table, state_gdn_s, state_gdn_conv, state_rwkv_shift, state_rwkv_wkv, norm_mix, norm_ffn, norm_final, fox_w_in, fox_b_f, fox_q_norm, fox_k_norm, fox_w_out, gdn_w_in, gdn_conv_w, gdn_a_log, gdn_dt_bias, gdn_out_norm, gdn_w_out, rwkv_mu, rwkv_w0, rwkv_w1, rwkv_w2, rwkv_a0, rwkv_a1, rwkv_a2, rwkv_g1, rwkv_g2, rwkv_k_k, rwkv_k_a, rwkv_r_k, rwkv_w_r, rwkv_w_k, rwkv_w_v, rwkv_w_o, rwkv_ln_w, rwkv_ln_b, ffn_w_in, ffn_w_out):
    batch, seq, d = x_prompt.shape
    db = x_sample.shape[0]
    depth = norm_mix.shape[0]
    hp = x_prompt.reshape(batch * seq, d)
    hs = x_sample.reshape(db, d)
    cache_k_t = cache_k.transpose(0, 1, 3, 4, 2)
    cache_v_t = cache_v.transpose(0, 1, 3, 4, 2)
    cache_logf_t = cache_logf.transpose(0, 1, 3, 2)
    fox_p, fox_s, gdn_p, gdn_s, rwkv_p, rwkv_s = [], [], [], [], [], []
    for layer in range(depth):
        kind, j = layer % 3, layer // 3
        if kind == 0:
            hp, hs, op, os_ = fox_layer(hp, hs, norm_mix[layer], fox_w_in[j], fox_b_f[j], fox_q_norm[j], fox_k_norm[j],
                                        fox_w_out[j], j, cache_k_t, cache_v_t, cache_logf_t, page_table, batch, seq)
            fox_p.append(op)
            fox_s.append(os_)
        elif kind == 1:
            hp, hs, op, os_ = gdn_layer(hp, hs, norm_mix[layer], gdn_w_in[j], gdn_conv_w[j], gdn_a_log[j],
                                        gdn_dt_bias[j], gdn_out_norm[j], gdn_w_out[j], state_gdn_conv[j],
                                        state_gdn_s[j], batch, seq)
            gdn_p.append(op)
            gdn_s.append(os_)
        else:
            hp, hs, op, os_ = rwkv_layer(hp, hs, norm_mix[layer], rwkv_mu[j], rwkv_w0[j], rwkv_w1[j], rwkv_w2[j],
                                         rwkv_a0[j], rwkv_a1[j], rwkv_a2[j], rwkv_g1[j], rwkv_g2[j], rwkv_k_k[j],
                                         rwkv_k_a[j], rwkv_r_k[j], rwkv_w_r[j], rwkv_w_k[j], rwkv_w_v[j],
                                         rwkv_w_o[j], rwkv_ln_w[j], rwkv_ln_b[j], state_rwkv_shift[j],
                                         state_rwkv_wkv[j], batch, seq)
            rwkv_p.append(op)
            rwkv_s.append(os_)
        final = norm_final if layer == depth - 1 else None
        w_in_b, w_out_b = _bf(ffn_w_in[layer]), _bf(ffn_w_out[layer])
        hp = ffn(hp, norm_ffn[layer], w_in_b, w_out_b, final)
        hs = ffn(hs, norm_ffn[layer], w_in_b, w_out_b, final)
    stack = lambda items, i: jnp.stack([it[i] for it in items])
    return (hp.reshape(batch, seq, d), hs.reshape(db, 1, d),
            stack(fox_p, 0), stack(fox_p, 1), stack(fox_p, 2),
            stack(fox_s, 0), stack(fox_s, 1), stack(fox_s, 2),
            stack(gdn_p, 0), stack(gdn_p, 1), stack(gdn_s, 0), stack(gdn_s, 1),
            stack(rwkv_p, 0), stack(rwkv_p, 1), stack(rwkv_s, 0), stack(rwkv_s, 1))
```

```python
import functools

import jax
import jax.numpy as jnp
from jax import lax
from jax.experimental import pallas as pl
from jax.experimental.pallas import tpu as pltpu

F32 = jnp.float32
BF16 = jnp.bfloat16
HIGHEST = lax.Precision.HIGHEST

NORM_EPS = 1e-6
RWKV_GN_EPS = 64e-5
LANES = 128
VMEM_LIMIT_BYTES = 48 * 1024 * 1024
NEG_BIG = -1e30

FOX_HEAD_DIM = 64
GDN_HEAD_DIM = 128
GDN_CONV_W = 4
RWKV_HEAD_DIM = 64
CHUNK = 64
FOX_Q_TILE = 256
FOX_K_TILE = 512
DECODE_PAGES_PER_STEP = 4


def _params(*semantics):
    return pltpu.CompilerParams(dimension_semantics=semantics, vmem_limit_bytes=VMEM_LIMIT_BYTES)


def _sigmoid(x):
    return 1.0 / (1.0 + jnp.exp(-x))


def _rms(x, gain):
    return x * lax.rsqrt(jnp.mean(x * x, axis=-1, keepdims=True) + NORM_EPS) * gain


def _dot(a, b):
    return jnp.dot(a, b, preferred_element_type=F32)


def _dot_nt(a, b):
    return lax.dot_general(a, b, (((1,), (1,)), ((), ())), preferred_element_type=F32)


def _dot_tn(a, b):
    return lax.dot_general(a, b, (((0,), (0,)), ((), ())), preferred_element_type=F32)


def _dot_hi(a, b):
    return jnp.dot(a, b, preferred_element_type=F32, precision=HIGHEST)


def _dot_nt_hi(a, b):
    return lax.dot_general(a, b, (((1,), (1,)), ((), ())), preferred_element_type=F32, precision=HIGHEST)


def _dot_tn_hi(a, b):
    return lax.dot_general(a, b, (((0,), (0,)), ((), ())), preferred_element_type=F32, precision=HIGHEST)


def _bf(x):
    return x.astype(BF16)


def _top_bits(x):
    bits = lax.bitcast_convert_type(x, jnp.int32) & jnp.int32(-65536)
    return lax.bitcast_convert_type(bits, F32)


def _split2(x):
    hi = _top_bits(x)
    return _bf(hi), _bf(x - hi)


def _split3(x):
    hi = _top_bits(x)
    r = x - hi
    mid = _top_bits(r)
    return _bf(hi), _bf(mid), _bf(r - mid)


def _dot_x3(a2, b2):
    (ah, al), (bh, bl) = a2, b2
    return _dot(ah, bh) + (_dot(ah, bl) + _dot(al, bh))


def _pad_cols(w, n):
    return jnp.pad(w, ((0, 0), (0, n - w.shape[1])))


def _pad_rows(w, n):
    return jnp.pad(w, ((0, n - w.shape[0]), (0, 0)))


def _col_tile(n, cap=1536):
    best = LANES
    for t in range(LANES, min(n, cap) + 1, LANES):
        if n % t == 0:
            best = t
    return best


def _mm_kernel(*refs, norm, residual):
    it = iter(refs)
    x_ref = next(it)
    g_ref = next(it) if norm else None
    w_ref = next(it)
    r_ref = next(it) if residual else None
    o_ref = next(it)
    xn_ref = next(it) if norm else None
    if norm:
        @pl.when(pl.program_id(1) == 0)
        def _():
            xn_ref[...] = _bf(_rms(x_ref[...], g_ref[...]))
        a = xn_ref[...]
    else:
        a = _bf(x_ref[...])
    acc = _dot(a, w_ref[...])
    if residual:
        acc = acc + r_ref[...]
    o_ref[...] = acc.astype(o_ref.dtype)


def mm(x, w, *, gain=None, res=None, out_dtype=F32):
    m, k = x.shape
    n = w.shape[1]
    tm = min(m, 512)
    tn = _col_tile(n)
    norm, residual = gain is not None, res is not None
    in_specs = [pl.BlockSpec((tm, k), lambda i, j: (i, 0))]
    args = [x]
    if norm:
        in_specs.append(pl.BlockSpec((1, k), lambda i, j: (0, 0)))
        args.append(gain.reshape(1, k))
    in_specs.append(pl.BlockSpec((k, tn), lambda i, j: (0, j)))
    args.append(w)
    if residual:
        in_specs.append(pl.BlockSpec((tm, tn), lambda i, j: (i, j)))
        args.append(res)
    return pl.pallas_call(
        functools.partial(_mm_kernel, norm=norm, residual=residual),
        grid=(m // tm, n // tn),
        in_specs=in_specs,
        out_specs=pl.BlockSpec((tm, tn), lambda i, j: (i, j)),
        out_shape=jax.ShapeDtypeStruct((m, n), out_dtype),
        scratch_shapes=[pltpu.VMEM((tm, k), BF16)] if norm else [],
        compiler_params=_params("parallel", "arbitrary"),
        name="dense",
    )(*args)


def _mm_t_kernel(x_ref, g_ref, w_ref, o_ref, xn_ref):
    @pl.when(pl.program_id(1) == 0)
    def _():
        xn_ref[...] = _bf(_rms(x_ref[...], g_ref[...]))

    o_ref[...] = _dot_nt(w_ref[...], xn_ref[...])


def mm_t(x, w_t, gain, *, batch, seq):
    m, k = x.shape
    n = w_t.shape[0]
    tm = min(seq, 512)
    tn = min(n, 1024)
    nt = seq // tm
    return pl.pallas_call(
        _mm_t_kernel,
        grid=(m // tm, n // tn),
        in_specs=[pl.BlockSpec((tm, k), lambda i, j: (i, 0)),
                  pl.BlockSpec((1, k), lambda i, j: (0, 0)),
                  pl.BlockSpec((tn, k), lambda i, j: (j, 0))],
        out_specs=pl.BlockSpec((None, tn, tm), lambda i, j: (i // nt, j, i % nt)),
        out_shape=jax.ShapeDtypeStruct((batch, n, seq), F32),
        scratch_shapes=[pltpu.VMEM((tm, k), BF16)],
        compiler_params=_params("parallel", "arbitrary"),
        name="dense_t",
    )(x, gain.reshape(1, k), w_t)


def _ffn_kernel(*refs, final):
    if final:
        x_ref, g_ref, wg_ref, wu_ref, wo_ref, fg_ref, o_ref, xn_ref, acc_ref = refs
    else:
        x_ref, g_ref, wg_ref, wu_ref, wo_ref, o_ref, xn_ref, acc_ref = refs
    k = pl.program_id(1)

    @pl.when(k == 0)
    def _():
        xn_ref[...] = _bf(_rms(x_ref[...], g_ref[...]))
        acc_ref[...] = jnp.zeros_like(acc_ref)

    xn = xn_ref[...]
    gate = _dot(xn, wg_ref[...])
    up = _dot(xn, wu_ref[...])
    act = _bf(gate * _sigmoid(gate) * up)
    acc_ref[...] += _dot(act, wo_ref[...])

    @pl.when(k == pl.num_programs(1) - 1)
    def _():
        out = x_ref[...] + acc_ref[...]
        if final:
            out = _rms(out, fg_ref[...])
        o_ref[...] = out


def ffn(x, gain, w_in, w_out, final_gain=None):
    m, d = x.shape
    hidden = w_out.shape[0]
    th = 256
    nk = hidden // th
    tm = min(m, 512)
    final = final_gain is not None
    in_specs = [
        pl.BlockSpec((tm, d), lambda i, k: (i, 0)),
        pl.BlockSpec((1, d), lambda i, k: (0, 0)),
        pl.BlockSpec((d, th), lambda i, k: (0, k)),
        pl.BlockSpec((d, th), lambda i, k: (0, k + nk)),
        pl.BlockSpec((th, d), lambda i, k: (k, 0)),
    ]
    args = [x, gain.reshape(1, d), w_in, w_in, w_out]
    if final:
        in_specs.append(pl.BlockSpec((1, d), lambda i, k: (0, 0)))
        args.append(final_gain.reshape(1, d))
    return pl.pallas_call(
        functools.partial(_ffn_kernel, final=final),
        grid=(m // tm, nk),
        in_specs=in_specs,
        out_specs=pl.BlockSpec((tm, d), lambda i, k: (i, 0)),
        out_shape=jax.ShapeDtypeStruct((m, d), F32),
        scratch_shapes=[pltpu.VMEM((tm, d), BF16), pltpu.VMEM((tm, d), F32)],
        compiler_params=_params("parallel", "arbitrary"),
        name="swiglu",
    )(*args)


def _fox_flash_kernel(q_ref, k_ref, vt_ref, g_ref, o_ref, m_ref, acc_ref, s_buf, p_buf, a_buf, *, tq, tk):
    i = pl.program_id(2)
    dh = FOX_HEAD_DIM
    heads = range(2)
    n_full = (i * tq) // tk
    m_ref[...] = jnp.full_like(m_ref, NEG_BIG)
    acc_ref[...] = jnp.zeros_like(acc_ref)
    p_buf[...] = jnp.zeros_like(p_buf)
    a_buf[...] = jnp.ones_like(a_buf)
    key = lax.broadcasted_iota(jnp.int32, (tk, tq), 0)
    qry = lax.broadcasted_iota(jnp.int32, (tk, tq), 1)

    def scores(j, slot):
        off = pl.multiple_of(j * tk, tk)
        for hh in heads:
            s_buf[slot, hh] = _dot(k_ref[pl.ds(off, tk), hh * LANES:(hh + 1) * LANES], q_ref[hh])

    def softmax(j, slot, masked):
        ss = [s_buf[slot, hh] for hh in heads]
        if masked:
            ss = [jnp.where(key + j * tk <= qry + i * tq, s, NEG_BIG) for s in ss]
        m_prev = [m_ref[hh] for hh in heads]
        m_new = [jnp.maximum(m_prev[hh], jnp.max(ss[hh], axis=0, keepdims=True)) for hh in heads]
        for hh in heads:
            p_buf[slot, hh] = _bf(jnp.exp(ss[hh] - m_new[hh]))
            a_buf[slot, hh] = jnp.exp(m_prev[hh] - m_new[hh])
            m_ref[hh] = m_new[hh]

    def values(j, slot):
        off = pl.multiple_of(jnp.maximum(j, 0) * tk, tk)
        for hh in heads:
            acc_ref[hh] = a_buf[slot, hh] * acc_ref[hh] + _dot(vt_ref[hh, :, pl.ds(off, tk)], p_buf[slot, hh])

    scores(0, 0)

    def pair(u, carry):
        j = 2 * u
        scores(j + 1, 1)
        softmax(j, 0, False)
        values(j - 1, 1)
        scores(j + 2, 0)
        softmax(j + 1, 1, False)
        values(j, 0)
        return carry

    n_pairs = n_full // 2
    lax.fori_loop(0, n_pairs, pair, 0)
    j = 2 * n_pairs

    @pl.when(n_full - j == 1)
    def _():
        scores(j + 1, 1)
        softmax(j, 0, False)
        values(j - 1, 1)
        softmax(j + 1, 1, True)
        values(j, 0)
        values(j + 1, 1)

    @pl.when(n_full - j == 0)
    def _():
        softmax(j, 0, True)
        values(j - 1, 1)
        values(j, 0)

    halves = []
    for hh in range(2):
        a = acc_ref[hh].T
        halves.append(a[:, :dh] / a[:, dh:dh + 1])
    o_ref[...] = _bf(jnp.concatenate(halves, axis=-1) * _sigmoid(g_ref[...]))


def fox_flash(qt, ka, vt, gate_src, gate_col, *, batch, seq):
    m = ka.shape[0]
    n_pairs = vt.shape[1]
    tq = min(seq, FOX_Q_TILE)
    tk = min(seq, FOX_K_TILE)
    nq = seq // tq
    return pl.pallas_call(
        functools.partial(_fox_flash_kernel, tq=tq, tk=tk),
        grid=(batch, n_pairs, nq),
        in_specs=[
            pl.BlockSpec((None, None, 2, LANES, tq), lambda b, h, i: (b, h, 0, 0, i)),
            pl.BlockSpec((seq, 2 * LANES), lambda b, h, i: (b, h)),
            pl.BlockSpec((None, None, 2, LANES, seq), lambda b, h, i: (b, h, 0, 0, 0)),
            pl.BlockSpec((tq, LANES), lambda b, h, i: (b * nq + i, gate_col + h)),
        ],
        out_specs=pl.BlockSpec((tq, LANES), lambda b, h, i: (b * nq + i, h)),
        out_shape=jax.ShapeDtypeStruct((m, n_pairs * LANES), BF16),
        scratch_shapes=[pltpu.VMEM((2, 1, tq), F32), pltpu.VMEM((2, LANES, tq), F32),
                        pltpu.VMEM((2, 2, tk, tq), F32), pltpu.VMEM((2, 2, tk, tq), BF16),
                        pltpu.VMEM((2, 2, 1, tq), F32)],
        compiler_params=_params("parallel", "parallel", "arbitrary"),
        name="fox_flash",
    )(qt, ka, vt, gate_src)


def _fox_decode_kernel(pt_ref, q_ref, kn_ref, vn_ref, lfn_ref, g_ref, *refs, pps):
    lf_refs, k_refs, v_refs = refs[:pps], refs[pps:2 * pps], refs[2 * pps:3 * pps]
    o_ref, qb_ref, m_ref, l_ref, carry_ref, acc_ref = refs[3 * pps:]
    p = pl.program_id(1)
    n_heads, dh, page = k_refs[0].shape
    heads = range(n_heads)

    @pl.when(p == 0)
    def _():
        eye = (lax.broadcasted_iota(jnp.int32, (dh, dh), 0) == lax.broadcasted_iota(jnp.int32, (dh, dh), 1)).astype(F32)
        q_t = _dot_nt_hi(eye, q_ref[...])
        kn_t = _dot_nt_hi(eye, kn_ref[...])
        vn_t = _dot_nt_hi(eye, vn_ref[...])
        lane0 = lax.broadcasted_iota(jnp.int32, (dh, page), 1) == 0
        for h in heads:
            q_col = q_t[:, h:h + 1]
            qb_ref[h] = jnp.broadcast_to(q_col, (dh, page))
            s_new = jnp.sum(q_col * kn_t[:, h:h + 1], axis=0, keepdims=True)
            m_ref[h:h + 1, :] = jnp.broadcast_to(s_new, (1, page))
            acc_ref[h] = jnp.where(lane0, jnp.broadcast_to(vn_t[:, h:h + 1], (dh, page)), 0.0)
        l_ref[...] = jnp.ones_like(l_ref)
        carry_ref[...] = jnp.broadcast_to(lfn_ref[...], (n_heads, page))

    later = (lax.broadcasted_iota(jnp.int32, (page, page), 0) > lax.broadcasted_iota(jnp.int32, (page, page), 1)).astype(F32)
    carry = carry_ref[...]
    ss = []
    for i in range(pps):
        lf = lf_refs[i][...]
        bias = carry + _dot_hi(lf, later)
        carry = carry + jnp.sum(lf, axis=-1, keepdims=True)
        ss.append(jnp.concatenate([jnp.sum(qb_ref[h] * k_refs[i][h], axis=0, keepdims=True) for h in heads], axis=0)
                  + bias)
    carry_ref[...] = carry
    m_prev = m_ref[...]
    m_new = m_prev
    for s in ss:
        m_new = jnp.maximum(m_new, jnp.max(s, axis=-1, keepdims=True))
    alpha = jnp.exp(m_prev - m_new)
    prs = [jnp.exp(s - m_new) for s in ss]
    l_new = alpha * l_ref[...]
    for pr in prs:
        l_new = l_new + jnp.sum(pr, axis=-1, keepdims=True)
    l_ref[...] = l_new
    m_ref[...] = m_new
    for h in heads:
        acc = alpha[h:h + 1, :] * acc_ref[h]
        for i in range(pps):
            acc = acc + prs[i][h:h + 1, :] * v_refs[i][h]
        acc_ref[h] = acc

    @pl.when(p == pl.num_programs(1) - 1)
    def _():
        for h in heads:
            acc_ref[h] = acc_ref[h] / l_ref[h:h + 1, :]
        ones = jnp.ones((8, page), F32)
        o = _dot_nt_hi(ones, acc_ref[...].reshape(n_heads * dh, page))
        o_ref[...] = o[0:1, :] * g_ref[...]


def fox_decode(layer_idx, page_table, q, k_new, v_new, lf_new, gate, cache_logf_t, cache_k_t, cache_v_t):
    db, n_heads, dh = q.shape
    n_pages = page_table.shape[1]
    page = cache_k_t.shape[4]
    pps = DECODE_PAGES_PER_STEP if n_pages % DECODE_PAGES_PER_STEP == 0 else 1
    vec = pl.BlockSpec((None, n_heads, dh), lambda b, p, pt: (b, 0, 0))
    row = pl.BlockSpec((None, 1, n_heads * dh), lambda b, p, pt: (b, 0, 0))

    def past(i):
        return lambda b, p, pt: pt[b, n_pages - 1 - (p * pps + i)]

    lf_specs = [pl.BlockSpec((None, None, n_heads, page), lambda b, p, pt, f=past(i): (layer_idx, f(b, p, pt), 0, 0))
                for i in range(pps)]
    kv_specs = [pl.BlockSpec((None, None, n_heads, dh, page),
                             lambda b, p, pt, f=past(i): (layer_idx, f(b, p, pt), 0, 0, 0)) for i in range(pps)]
    return pl.pallas_call(
        functools.partial(_fox_decode_kernel, pps=pps),
        grid_spec=pltpu.PrefetchScalarGridSpec(
            num_scalar_prefetch=1,
            grid=(db, n_pages // pps),
            in_specs=[vec, vec, vec, pl.BlockSpec((None, n_heads, 1), lambda b, p, pt: (b, 0, 0)), row]
            + lf_specs + kv_specs + kv_specs,
            out_specs=row,
            scratch_shapes=[pltpu.VMEM((n_heads, dh, page), F32), pltpu.VMEM((n_heads, page), F32),
                            pltpu.VMEM((n_heads, page), F32), pltpu.VMEM((n_heads, page), F32),
                            pltpu.VMEM((n_heads, dh, page), F32)],
        ),
        out_shape=jax.ShapeDtypeStruct((db, 1, n_heads * dh), F32),
        compiler_params=_params("parallel", "arbitrary"),
        name="fox_decode",
    )(page_table, q, k_new, v_new, lf_new, gate, *([cache_logf_t] * pps), *([cache_k_t] * pps), *([cache_v_t] * pps))


def _tri_masks(c):
    row = lax.broadcasted_iota(jnp.int32, (c, c), 0)
    col = lax.broadcasted_iota(jnp.int32, (c, c), 1)
    return row >= col, row > col, row == col


def _cumsum_rows(tri_b, x):
    hi, mid, lo = _split3(x)
    return _dot(tri_b, hi) + (_dot(tri_b, mid) + _dot(tri_b, lo))


def _neumann_inverses(mats, eye, c):
    steps = max(c.bit_length() - 2, 0)
    ps = [eye + a for a in mats]
    if steps == 0:
        return ps
    splits = [_split2(a) for a in mats]
    aks = [_dot_x3(s, s) for s in splits]
    for step in range(steps):
        last = step == steps - 1
        nxt = []
        for p, ak in zip(ps, aks):
            ak2 = _split2(ak)
            lhs = p if last else jnp.concatenate([p, ak], axis=0)
            nxt.append(_dot_x3(_split2(lhs), ak2))
        ps = [p + n[:c] for p, n in zip(ps, nxt)]
        aks = [None if last else n[c:] for n in nxt]
    return ps


def _gdn_chunk_kernel(q_ref, k_ref, v_ref, z_ref, bcol_ref, gcol_ref, grow_ref, gain_ref, o_ref, s_out_ref, s_ref, *, c, hb):
    ch = pl.program_id(2)
    dk = GDN_HEAD_DIM

    @pl.when(ch == 0)
    def _():
        s_ref[...] = jnp.zeros_like(s_ref)

    incl, strict, diag = _tri_masks(c)
    eye = jnp.where(diag, 1.0, 0.0).astype(F32)
    tri_b = jnp.where(incl, 1.0, 0.0).astype(BF16)
    gc_col = _cumsum_rows(tri_b, gcol_ref[...])
    g3 = _split3(grow_ref[...])
    gc_row = _dot_nt(g3[0], tri_b) + (_dot_nt(g3[1], tri_b) + _dot_nt(g3[2], tri_b))
    heads = range(hb)
    vs = [slice(hh * dk, (hh + 1) * dk) for hh in heads]
    q = [q_ref[:, (hh // 2) * dk:(hh // 2 + 1) * dk] for hh in heads]
    k = [k_ref[:, (hh // 2) * dk:(hh // 2 + 1) * dk] for hh in heads]
    beta = [bcol_ref[:, hh:hh + 1] for hh in heads]
    gcc = [gc_col[:, hh:hh + 1] for hh in heads]
    g_last = [g[c - 1:c, :] for g in gcc]
    decay = [jnp.where(incl, jnp.exp(jnp.where(incl, gcc[hh] - gc_row[hh:hh + 1, :], 0.0)), 0.0) for hh in heads]
    kb = [k[hh] * beta[hh] for hh in heads]
    kbf = [_bf(x) for x in k]
    kk = [_dot_nt(_bf(kb[hh]), kbf[hh]) for hh in heads]
    qk = [_dot_nt(_bf(q[hh]), kbf[hh]) for hh in heads]
    neg_lower = [jnp.where(strict, -(kk[hh] * decay[hh]), 0.0) for hh in heads]
    attn = [_bf(jnp.where(incl, qk[hh] * decay[hh], 0.0)) for hh in heads]
    e_gc = [jnp.exp(g) for g in gcc]
    rhs = [_split2(jnp.concatenate([v_ref[:, vs[hh]] * beta[hh], kb[hh] * e_gc[hh]], axis=-1)) for hh in heads]
    q_dec = [_bf(q[hh] * e_gc[hh]) for hh in heads]
    k_dec = [_bf(k[hh] * jnp.exp(g_last[hh] - gcc[hh])) for hh in heads]
    t_inv = _neumann_inverses(neg_lower, eye, c)
    sol = [_dot_x3(_split2(t_inv[hh]), rhs[hh]) for hh in heads]
    s = [s_ref[hh] for hh in heads]
    sb = [_bf(x) for x in s]
    vnb = [_bf(sol[hh][:, :dk] - _dot(_bf(sol[hh][:, dk:]), sb[hh])) for hh in heads]
    o = [_dot(q_dec[hh], sb[hh]) + _dot(attn[hh], vnb[hh]) for hh in heads]
    s_new = [s[hh] * jnp.exp(g_last[hh]) + _dot_tn(k_dec[hh], vnb[hh]) for hh in heads]
    for hh in heads:
        s_ref[hh] = s_new[hh]
        z = z_ref[:, vs[hh]]
        o_ref[:, vs[hh]] = _bf(_rms(o[hh], gain_ref[...]) * (z * _sigmoid(z)))

    @pl.when(ch == pl.num_programs(2) - 1)
    def _():
        s_out_ref[...] = s_ref[...]


def gdn_chunk_scan(q, k, v, z, beta, g, gain, *, batch, seq):
    m = q.shape[0]
    n_heads = v.shape[1] // GDN_HEAD_DIM
    c = min(CHUNK, seq)
    hb = min(8, n_heads)
    n_groups = n_heads // hb
    n = seq // c
    bcol = beta.reshape(m, n_groups, hb).transpose(1, 0, 2)
    gcol = g.reshape(m, n_groups, hb).transpose(1, 0, 2)
    grow = g.reshape(batch * n, c, n_groups, hb).transpose(0, 2, 3, 1)
    qk_w = (hb // 2) * GDN_HEAD_DIM
    v_w = hb * GDN_HEAD_DIM
    row_blk = lambda b, h, t: (b * n + t, h)
    col_spec = pl.BlockSpec((None, c, hb), lambda b, h, t: (h, b * n + t, 0))
    return pl.pallas_call(
        functools.partial(_gdn_chunk_kernel, c=c, hb=hb),
        grid=(batch, n_groups, n),
        in_specs=[
            pl.BlockSpec((c, qk_w), row_blk),
            pl.BlockSpec((c, qk_w), row_blk),
            pl.BlockSpec((c, v_w), row_blk),
            pl.BlockSpec((c, v_w), row_blk),
            col_spec, col_spec,
            pl.BlockSpec((None, None, hb, c), lambda b, h, t: (b * n + t, h, 0, 0)),
            pl.BlockSpec((1, GDN_HEAD_DIM), lambda b, h, t: (0, 0)),
        ],
        out_specs=[
            pl.BlockSpec((c, v_w), row_blk),
            pl.BlockSpec((None, hb, GDN_HEAD_DIM, GDN_HEAD_DIM), lambda b, h, t: (b, h, 0, 0)),
        ],
        out_shape=[jax.ShapeDtypeStruct((m, n_heads * GDN_HEAD_DIM), BF16),
                   jax.ShapeDtypeStruct((batch, n_heads, GDN_HEAD_DIM, GDN_HEAD_DIM), F32)],
        scratch_shapes=[pltpu.VMEM((hb, GDN_HEAD_DIM, GDN_HEAD_DIM), F32)],
        compiler_params=_params("parallel", "parallel", "arbitrary"),
        name="gdn_chunk_scan",
    )(q, k, v, z, bcol, gcol, grow, gain.reshape(1, GDN_HEAD_DIM))


def _gdn_step_kernel(q_ref, k_ref, v_ref, z_ref, beta_ref, g_ref, gain_ref, s0_ref, o_ref, s_ref):
    dk = GDN_HEAD_DIM
    n_heads = s0_ref.shape[0]
    n_qk = q_ref.shape[0]
    rep = n_heads // n_qk
    q_rows = q_ref[...]
    k_rows = k_ref[...]
    eye = (lax.broadcasted_iota(jnp.int32, (dk, dk), 0) == lax.broadcasted_iota(jnp.int32, (dk, dk), 1)).astype(F32)
    cols = _dot_nt_hi(eye, jnp.concatenate([q_rows, k_rows], axis=0))
    qk_dot = jnp.sum(q_rows * k_rows, axis=-1, keepdims=True)
    for j in range(n_qk):
        q_c = jnp.broadcast_to(cols[:, j:j + 1], (dk, dk))
        k_c = jnp.broadcast_to(cols[:, n_qk + j:n_qk + j + 1], (dk, dk))
        for h in range(j * rep, (j + 1) * rep):
            vs = slice(h * dk, (h + 1) * dk)
            beta = beta_ref[:, h:h + 1]
            e_g = jnp.exp(g_ref[:, h:h + 1])
            s0 = s0_ref[h]
            w_s = jnp.sum(s0 * k_c, axis=0, keepdims=True) * (beta * e_g)
            q_s = jnp.sum(s0 * q_c, axis=0, keepdims=True) * e_g
            v_new = v_ref[:, vs] * beta - w_s
            o = q_s + qk_dot[j:j + 1, :] * v_new
            s_ref[h] = s0 * e_g + k_c * v_new
            z = z_ref[:, vs]
            o_ref[:, vs] = _rms(o, gain_ref[...]) * (z * _sigmoid(z))


def gdn_step(q, k, v, z, beta, g, gain, s0):
    db, n_qk, dk = q.shape
    n_heads = s0.shape[1]
    vec = lambda w: pl.BlockSpec((None, 1, w), lambda b: (b, 0, 0))
    rows = pl.BlockSpec((None, n_qk, dk), lambda b: (b, 0, 0))
    st = pl.BlockSpec((None, n_heads, dk, dk), lambda b: (b, 0, 0, 0))
    return pl.pallas_call(
        _gdn_step_kernel,
        grid=(db,),
        in_specs=[rows, rows, vec(v.shape[2]), vec(v.shape[2]), vec(n_heads), vec(n_heads),
                  pl.BlockSpec((1, dk), lambda b: (0, 0)), st],
        out_specs=[vec(v.shape[2]), st],
        out_shape=[jax.ShapeDtypeStruct(v.shape, F32), jax.ShapeDtypeStruct(s0.shape, F32)],
        compiler_params=_params("parallel"),
        name="gdn_step",
    )(q, k, v, z, beta, g, gain.reshape(1, dk), s0)


def _rwkv_post(y, r, k, v, gate, rk, lnw, lnb):
    mean = jnp.mean(y, axis=-1, keepdims=True)
    var = jnp.mean(jnp.square(y - mean), axis=-1, keepdims=True)
    y = (y - mean) * lax.rsqrt(var + RWKV_GN_EPS) * lnw + lnb
    y = y + jnp.sum(r * k * rk, axis=-1, keepdims=True) * v
    return y * gate


def _rwkv_keys(k, a, kk_w, ka_w):
    kk = k * kk_w
    kk = kk * lax.rsqrt(jnp.sum(kk * kk, axis=-1, keepdims=True) + 1e-6)
    k = k * (1.0 + (a - 1.0) * ka_w)
    return k, -kk, kk * a


def _rwkv_chunk_kernel(r_ref, lw_ref, k_ref, v_ref, a_ref, gate_ref, kk_ref, ka_ref, rk_ref, lnw_ref, lnb_ref,
                       o_ref, s_out_ref, s_ref, *, c, hb):
    ch = pl.program_id(2)
    n = RWKV_HEAD_DIM

    @pl.when(ch == 0)
    def _():
        s_ref[...] = jnp.zeros_like(s_ref)

    incl, strict, diag = _tri_masks(c)
    eye = jnp.where(diag, 1.0, 0.0).astype(F32)
    tri_b = jnp.where(incl, 1.0, 0.0).astype(BF16)
    gi_all = _cumsum_rows(tri_b, lw_ref[...])
    heads = range(hb)
    sl = [slice(hh * n, (hh + 1) * n) for hh in heads]
    r = [r_ref[:, s_] for s_ in sl]
    v = [v_ref[:, s_] for s_ in sl]
    vb = [_bf(x) for x in v]
    keys = [_rwkv_keys(k_ref[:, s_], a_ref[:, s_], kk_ref[:, s_], ka_ref[:, s_]) for s_ in sl]
    k = [t[0] for t in keys]
    a_vec = [t[1] for t in keys]
    b_vec = [t[2] for t in keys]
    gi = [gi_all[:, s_] for s_ in sl]
    g_last = [g[c - 1:c, :] for g in gi]
    e_neg = [jnp.exp(-g) for g in gi]
    e_rest = [jnp.exp(g_last[hh] - gi[hh]) for hh in heads]
    a_t = [a_vec[hh] * jnp.exp(gi[hh] - lw_ref[:, sl[hh]]) for hh in heads]
    r_tb = [_bf(r[hh] * jnp.exp(gi[hh])) for hh in heads]
    lhs = [jnp.concatenate([_bf(a_t[hh]), r_tb[hh]], axis=0) for hh in heads]
    rhs = [_bf(jnp.concatenate([b_vec[hh] * e_neg[hh], k[hh] * e_neg[hh]], axis=0)) for hh in heads]
    tail = [_bf(jnp.concatenate([b_vec[hh] * e_rest[hh], k[hh] * e_rest[hh]], axis=0)) for hh in heads]
    m4 = [_dot_nt(lhs[hh], rhs[hh]) for hh in heads]
    a_ab = [jnp.where(strict, m[:c, :c], 0.0) for m in m4]
    a_ak = [_bf(jnp.where(strict, m[:c, c:], 0.0)) for m in m4]
    a_rb = [_bf(jnp.where(incl, m[c:, :c], 0.0)) for m in m4]
    a_rk = [_bf(jnp.where(incl, m[c:, c:], 0.0)) for m in m4]
    akv = [_dot(a_ak[hh], vb[hh]) for hh in heads]
    t_inv = _neumann_inverses(a_ab, eye, c)
    sol = [_dot_x3(_split2(t_inv[hh]), _split2(jnp.concatenate([a_t[hh], akv[hh]], axis=-1))) for hh in heads]
    s = [s_ref[hh] for hh in heads]
    sb = [_bf(x) for x in s]
    ub = [_bf(_dot_nt(_bf(sol[hh][:, :n]), sb[hh]) + sol[hh][:, n:]) for hh in heads]
    y = [_dot_nt(r_tb[hh], sb[hh]) + _dot(a_rb[hh], ub[hh]) + _dot(a_rk[hh], vb[hh]) for hh in heads]
    s_new = [s[hh] * jnp.exp(g_last[hh]) + _dot_tn(jnp.concatenate([ub[hh], vb[hh]], axis=0), tail[hh])
             for hh in heads]
    for hh in heads:
        s_ref[hh] = s_new[hh]
        o_ref[:, sl[hh]] = _bf(_rwkv_post(y[hh], r[hh], k[hh], v[hh], gate_ref[:, sl[hh]], rk_ref[:, sl[hh]],
                                          lnw_ref[:, sl[hh]], lnb_ref[:, sl[hh]]))

    @pl.when(ch == pl.num_programs(2) - 1)
    def _():
        s_out_ref[...] = s_ref[...]


def rwkv_chunk_scan(r, lw, k, v, a, gate, kk_w, ka_w, rk, lnw, lnb, *, batch, seq):
    m, d = r.shape
    n_heads = d // RWKV_HEAD_DIM
    hb = min(8, n_heads)
    w = hb * RWKV_HEAD_DIM
    c = min(CHUNK, seq)
    n = seq // c
    tok = pl.BlockSpec((c, w), lambda b, h, t: (b * n + t, h))
    par = pl.BlockSpec((1, w), lambda b, h, t: (0, h))
    return pl.pallas_call(
        functools.partial(_rwkv_chunk_kernel, c=c, hb=hb),
        grid=(batch, n_heads // hb, n),
        in_specs=[tok] * 6 + [par] * 5,
        out_specs=[tok, pl.BlockSpec((None, hb, RWKV_HEAD_DIM, RWKV_HEAD_DIM), lambda b, h, t: (b, h, 0, 0))],
        out_shape=[jax.ShapeDtypeStruct((m, d), BF16),
                   jax.ShapeDtypeStruct((batch, n_heads, RWKV_HEAD_DIM, RWKV_HEAD_DIM), F32)],
        scratch_shapes=[pltpu.VMEM((hb, RWKV_HEAD_DIM, RWKV_HEAD_DIM), F32)],
        compiler_params=_params("parallel", "parallel", "arbitrary"),
        name="rwkv_chunk_scan",
    )(r, lw, k, v, a, gate, kk_w, ka_w, rk, lnw, lnb)


def _rwkv_step_kernel(r_ref, lw_ref, k_ref, v_ref, a_ref, gate_ref, kk_ref, ka_ref, rk_ref, lnw_ref, lnb_ref,
                      s0_ref, o_ref, s_ref, y_ref):
    n = s0_ref.shape[0]
    r = r_ref[...]
    v = v_ref[...]
    k0 = k_ref[...]
    a = a_ref[...]
    kk = k0 * kk_ref[...]
    kk = kk * lax.rsqrt(jnp.sum(kk * kk, axis=0, keepdims=True) + 1e-6)
    k = k0 * (1.0 + (a - 1.0) * ka_ref[...])
    a_vec = -kk
    b_vec = kk * a
    w = jnp.exp(lw_ref[...])

    def body(i, carry):
        s0 = s0_ref[i]
        sa = jnp.sum(s0 * a_vec, axis=0, keepdims=True)
        s = s0 * w + sa * b_vec + v_ref[pl.ds(i, 1), :] * k
        s_ref[i] = s
        y_ref[pl.ds(i, 1), :] = jnp.sum(s * r, axis=0, keepdims=True)
        return carry

    lax.fori_loop(0, n, body, 0, unroll=8)
    y = y_ref[...]
    mean = jnp.mean(y, axis=0, keepdims=True)
    var = jnp.mean(jnp.square(y - mean), axis=0, keepdims=True)
    y = (y - mean) * lax.rsqrt(var + RWKV_GN_EPS) * lnw_ref[...] + lnb_ref[...]
    y = y + jnp.sum(r * k * rk_ref[...], axis=0, keepdims=True) * v
    o_ref[...] = y * gate_ref[...]


def rwkv_step(r, lw, k, v, a, gate, kk_w, ka_w, rk, lnw, lnb, s0_t):
    d, db = r.shape
    n = RWKV_HEAD_DIM
    n_heads = d // n
    tok = pl.BlockSpec((n, db), lambda h: (h, 0))
    par = pl.BlockSpec((n, 1), lambda h: (h, 0))
    st = pl.BlockSpec((None, n, n, db), lambda h: (h, 0, 0, 0))
    return pl.pallas_call(
        _rwkv_step_kernel,
        grid=(n_heads,),
        in_specs=[tok] * 6 + [par] * 5 + [st],
        out_specs=[tok, st],
        out_shape=[jax.ShapeDtypeStruct((d, db), F32), jax.ShapeDtypeStruct(s0_t.shape, F32)],
        scratch_shapes=[pltpu.VMEM((n, db), F32)],
        compiler_params=_params("parallel"),
        name="rwkv_step",
    )(r, lw, k, v, a, gate, kk_w, ka_w, rk, lnw, lnb, s0_t)


def _head_rms(x, gain, n_heads):
    m = x.shape[0]
    xh = x.reshape(m, n_heads, -1)
    return (xh * lax.rsqrt(jnp.mean(xh * xh, axis=-1, keepdims=True) + NORM_EPS) * gain).reshape(m, -1)


def _fox_project(x, gain, w_in, b_f, q_gain, k_gain):
    width = (w_in.shape[1] // LANES) * LANES
    n_heads = w_in.shape[1] - width
    w = _bf(jnp.concatenate([w_in[:, :width], _pad_cols(w_in[:, width:], LANES)], axis=1))
    h = mm(x, w, gain=gain)
    d = width // 4
    q = _head_rms(h[:, :d], q_gain, n_heads) * (FOX_HEAD_DIM ** -0.5)
    k = _head_rms(h[:, d:2 * d], k_gain, n_heads)
    v = h[:, 2 * d:3 * d]
    gate = _sigmoid(h[:, 3 * d:4 * d])
    log_f = jax.nn.log_sigmoid(h[:, width:width + n_heads] + b_f)
    return q, k, v, gate, log_f


def _fox_prompt(x, gain, w_in, b_f, q_gain, k_gain, batch, seq):
    m = batch * seq
    dh = FOX_HEAD_DIM
    width = (w_in.shape[1] // LANES) * LANES
    n_heads = w_in.shape[1] - width
    d = width // 4
    qv_t = mm_t(x, _bf(jnp.concatenate([w_in[:, :d], w_in[:, 2 * d:3 * d]], axis=1).T), gain, batch=batch, seq=seq)
    rest = jnp.concatenate([w_in[:, d:2 * d], w_in[:, 3 * d:4 * d], _pad_cols(w_in[:, width:], 2 * LANES)], axis=1)
    h = mm(x, _bf(rest), gain=gain)
    k = _head_rms(h[:, :d], k_gain, n_heads)
    lf = jax.nn.log_sigmoid(h[:, 2 * d:2 * d + n_heads] + b_f)
    cum = jnp.cumsum(lf.reshape(batch, seq, n_heads), axis=1).reshape(m, n_heads)
    q_t = qv_t[:, :d].reshape(batch, n_heads, dh, seq)
    q_t = q_t * lax.rsqrt(jnp.mean(q_t * q_t, axis=2, keepdims=True) + NORM_EPS) * (q_gain * dh ** -0.5)[:, None]
    v_t = qv_t[:, d:].reshape(batch, n_heads, dh, seq)
    rows = lambda n, val: jnp.full((batch, n_heads, n, seq), val, BF16)
    qt = jnp.concatenate([_bf(q_t), rows(3, 1.0), rows(LANES - dh - 3, 0.0)], axis=2)
    vt = jnp.concatenate([_bf(v_t), rows(1, 1.0), rows(LANES - dh - 1, 0.0)], axis=2)
    neg_c3 = jnp.stack(_split3(-cum), axis=-1)
    ka = jnp.concatenate([_bf(k).reshape(m, n_heads, dh), neg_c3, jnp.zeros((m, n_heads, LANES - dh - 3), BF16)],
                         axis=-1).reshape(m, n_heads * LANES)
    pair = lambda a: a.reshape(batch, n_heads // 2, 2, LANES, seq)
    return pair(qt), ka, pair(vt), h, d // LANES, k, v_t.transpose(0, 3, 1, 2), lf


def fox_layer(hp, hs, gain, w_in, b_f, q_gain, k_gain, w_out, layer_idx, cache_k_t, cache_v_t, cache_logf_t,
              page_table, batch, seq):
    n_heads = b_f.shape[0]
    dh = FOX_HEAD_DIM
    w_out_b = _bf(w_out)
    qt, ka, vt, gate_src, gate_col, k, v, lf = _fox_prompt(hp, gain, w_in, b_f, q_gain, k_gain, batch, seq)
    og = fox_flash(qt, ka, vt, gate_src, gate_col, batch=batch, seq=seq)
    hp = mm(og, w_out_b, res=hp)
    outs_p = (k.reshape(batch, seq, n_heads, dh), v, lf.reshape(batch, seq, n_heads))
    db = hs.shape[0]
    q, k, v, gate, lf = _fox_project(hs, gain, w_in, b_f, q_gain, k_gain)
    og = fox_decode(layer_idx, page_table, q.reshape(db, n_heads, dh), k.reshape(db, n_heads, dh),
                    v.reshape(db, n_heads, dh), lf.reshape(db, n_heads, 1), gate.reshape(db, 1, n_heads * dh),
                    cache_logf_t, cache_k_t, cache_v_t)
    hs = mm(og.reshape(db, n_heads * dh), w_out_b, res=hs)
    outs_s = (k.reshape(db, 1, n_heads, dh), v.reshape(db, 1, n_heads, dh), lf.reshape(db, 1, n_heads))
    return hp, hs, outs_p, outs_s


def _gdn_project(x, gain, w_in, n_heads):
    total = w_in.shape[1]
    main = total - 2 * n_heads
    w = _bf(jnp.concatenate([w_in[:, :main], _pad_cols(w_in[:, main:], LANES)], axis=1))
    h = mm(x, w, gain=gain)
    conv_ch = main - n_heads * GDN_HEAD_DIM
    return h[:, :conv_ch], h[:, conv_ch:main], h[:, main:main + n_heads], h[:, main + n_heads:main + 2 * n_heads]


def _gdn_post_conv(conv, a, beta_logit, a_log, dt_bias, n_heads):
    m = conv.shape[0]
    conv = conv * _sigmoid(conv)
    qk_w = (n_heads // 2) * GDN_HEAD_DIM

    def l2(x):
        xh = x.reshape(m, n_heads // 2, GDN_HEAD_DIM)
        return (xh * lax.rsqrt(jnp.sum(xh * xh, axis=-1, keepdims=True) + 1e-6)).reshape(m, qk_w)

    q = l2(conv[:, :qk_w]) * (GDN_HEAD_DIM ** -0.5)
    k = l2(conv[:, qk_w:2 * qk_w])
    v = conv[:, 2 * qk_w:]
    beta = _sigmoid(beta_logit)
    g = -jnp.exp(a_log) * jax.nn.softplus(a + dt_bias)
    return q, k, v, beta, g


def gdn_layer(hp, hs, gain, w_in, conv_w, a_log, dt_bias, out_gain, w_out, conv_state, s_state, batch, seq):
    n_heads = a_log.shape[0]
    w_out_b = _bf(w_out)
    qkv, z, a, bl = _gdn_project(hp, gain, w_in, n_heads)
    ch = qkv.shape[1]
    xc = jnp.concatenate([jnp.zeros((batch, GDN_CONV_W - 1, ch), F32), qkv.reshape(batch, seq, ch)], axis=1)
    conv = xc[:, 0:seq] * conv_w[0]
    for i in range(1, GDN_CONV_W):
        conv = conv + xc[:, i:i + seq] * conv_w[i]
    q, k, v, beta, g = _gdn_post_conv(conv.reshape(batch * seq, ch), a, bl, a_log, dt_bias, n_heads)
    og, s_p = gdn_chunk_scan(q, k, v, z, beta, g, out_gain, batch=batch, seq=seq)
    hp = mm(og, w_out_b, res=hp)
    conv_p = xc[:, seq:]
    db = hs.shape[0]
    qkv, z, a, bl = _gdn_project(hs, gain, w_in, n_heads)
    xc = jnp.concatenate([conv_state, qkv[:, None, :]], axis=1)
    conv = xc[:, 0] * conv_w[0]
    for i in range(1, GDN_CONV_W):
        conv = conv + xc[:, i] * conv_w[i]
    q, k, v, beta, g = _gdn_post_conv(conv, a, bl, a_log, dt_bias, n_heads)
    og, s_s = gdn_step(q.reshape(db, -1, GDN_HEAD_DIM), k.reshape(db, -1, GDN_HEAD_DIM), v[:, None], z[:, None],
                       beta[:, None], g[:, None], out_gain, s_state)
    hs = mm(og.reshape(db, -1), w_out_b, res=hs)
    conv_s = xc[:, 1:]
    return hp, hs, (s_p, conv_p), (s_s, conv_s)


def _rwkv_project(u, u_prev, mu, w0, w1, w2, a0, a1, a2, g1, g2, w_r, w_k, w_v):
    du = u_prev - u
    xr, xw, xk, xv, xa, xg = (u + du * mu[i] for i in range(6))
    lora = lambda w: -(-w.shape[1] // LANES) * LANES
    r = mm(xr, _bf(w_r))
    k = mm(xk, _bf(w_k))
    v = mm(xv, _bf(w_v))
    nw, na, ng = lora(w1), lora(a1), lora(g1)
    w_l = mm(jnp.tanh(mm(xw, _bf(_pad_cols(w1, nw)))), _bf(_pad_rows(w2, nw)))
    a_l = mm(mm(xa, _bf(_pad_cols(a1, na))), _bf(_pad_rows(a2, na)))
    gate = mm(_sigmoid(mm(xg, _bf(_pad_cols(g1, ng)))), _bf(_pad_rows(g2, ng)))
    w_raw = -jax.nn.softplus(-(w0 + w_l)) - 0.5
    log_decay = -jnp.exp(w_raw)
    a = _sigmoid(a0 + a_l)
    return r, log_decay, k, v, a, gate


def rwkv_layer(hp, hs, gain, mu, w0, w1, w2, a0, a1, a2, g1, g2, k_k, k_a, r_k, w_r, w_k, w_v, w_o, ln_w, ln_b,
               shift_state, wkv_state, batch, seq):
    d = hp.shape[1]
    w_o_b = _bf(w_o)
    proj_w = (mu, w0, w1, w2, a0, a1, a2, g1, g2, w_r, w_k, w_v)
    chan = (k_k, k_a, r_k.reshape(d), ln_w, ln_b)
    u = _rms(hp, gain).reshape(batch, seq, d)
    u_prev = jnp.concatenate([jnp.zeros((batch, 1, d), F32), u[:, :-1]], axis=1)
    toks = _rwkv_project(u.reshape(batch * seq, d), u_prev.reshape(batch * seq, d), *proj_w)
    og, s_p = rwkv_chunk_scan(*toks, *(p.reshape(1, d) for p in chan), batch=batch, seq=seq)
    hp = mm(og, w_o_b, res=hp)
    shift_p = u[:, -1]
    us = _rms(hs, gain)
    toks = _rwkv_project(us, shift_state, *proj_w)
    og_t, s_t = rwkv_step(*(t.T for t in toks), *(p.reshape(d, 1) for p in chan), wkv_state.transpose(1, 2, 3, 0))
    hs = mm(og_t.T, w_o_b, res=hs)
    return hp, hs, (shift_p, s_p), (us, s_t.transpose(3, 0, 1, 2))


def kernel(x_prompt, x_sample, cache_k, cache_v, cache_logf, page_table, state_gdn_s, state_gdn_conv, state_rwkv_shift, state_rwkv_wkv, norm_mix, norm_ffn, norm_final, fox_w_in, fox_b_f, fox_q_norm, fox_k_norm, fox_w_out, gdn_w_in, gdn_conv_w, gdn_a_log, gdn_dt_bias, gdn_out_norm, gdn_w_out, rwkv_mu, rwkv_w0, rwkv_w1, rwkv_w2, rwkv_a0, rwkv_a1, rwkv_a2, rwkv_g1, rwkv_g2, rwkv_k_k, rwkv_k_a, rwkv_r_k, rwkv_w_r, rwkv_w_k, rwkv_w_v, rwkv_w_o, rwkv_ln_w, rwkv_ln_b, ffn_w_in, ffn_w_out):
    batch, seq, d = x_prompt.shape
    db = x_sample.shape[0]
    depth = norm_mix.shape[0]
    hp = x_prompt.reshape(batch * seq, d)
    hs = x_sample.reshape(db, d)
    cache_k_t = cache_k.transpose(0, 1, 3, 4, 2)
    cache_v_t = cache_v.transpose(0, 1, 3, 4, 2)
    cache_logf_t = cache_logf.transpose(0, 1, 3, 2)
    fox_p, fox_s, gdn_p, gdn_s, rwkv_p, rwkv_s = [], [], [], [], [], []
    for layer in range(depth):
        kind, j = layer % 3, layer // 3
        if kind == 0:
            hp, hs, op, os_ = fox_layer(hp, hs, norm_mix[layer], fox_w_in[j], fox_b_f[j], fox_q_norm[j], fox_k_norm[j],
                                        fox_w_out[j], j, cache_k_t, cache_v_t, cache_logf_t, page_table, batch, seq)
            fox_p.append(op)
            fox_s.append(os_)
        elif kind == 1:
            hp, hs, op, os_ = gdn_layer(hp, hs, norm_mix[layer], gdn_w_in[j], gdn_conv_w[j], gdn_a_log[j],
                                        gdn_dt_bias[j], gdn_out_norm[j], gdn_w_out[j], state_gdn_conv[j],
                                        state_gdn_s[j], batch, seq)
            gdn_p.append(op)
            gdn_s.append(os_)
        else:
            hp, hs, op, os_ = rwkv_layer(hp, hs, norm_mix[layer], rwkv_mu[j], rwkv_w0[j], rwkv_w1[j], rwkv_w2[j],
                                         rwkv_a0[j], rwkv_a1[j], rwkv_a2[j], rwkv_g1[j], rwkv_g2[j], rwkv_k_k[j],
                                         rwkv_k_a[j], rwkv_r_k[j], rwkv_w_r[j], rwkv_w_k[j], rwkv_w_v[j],
                                         rwkv_w_o[j], rwkv_ln_w[j], rwkv_ln_b[j], state_rwkv_shift[j],
                                         state_rwkv_wkv[j], batch, seq)
            rwkv_p.append(op)
            rwkv_s.append(os_)
        final = norm_final if layer == depth - 1 else None
        w_in_b, w_out_b = _bf(ffn_w_in[layer]), _bf(ffn_w_out[layer])
        hp = ffn(hp, norm_ffn[layer], w_in_b, w_out_b, final)
        hs = ffn(hs, norm_ffn[layer], w_in_b, w_out_b, final)
    stack = lambda items, i: jnp.stack([it[i] for it in items])
    return (hp.reshape(batch, seq, d), hs.reshape(db, 1, d),
            stack(fox_p, 0), stack(fox_p, 1), stack(fox_p, 2),
            stack(fox_s, 0), stack(fox_s, 1), stack(fox_s, 2),
            stack(gdn_p, 0), stack(gdn_p, 1), stack(gdn_s, 0), stack(gdn_s, 1),
            stack(rwkv_p, 0), stack(rwkv_p, 1), stack(rwkv_s, 0), stack(rwkv_s, 1))
```

```python
import functools

import jax
import jax.numpy as jnp
from jax import lax
from jax.experimental import pallas as pl
from jax.experimental.pallas import tpu as pltpu

F32 = jnp.float32
BF16 = jnp.bfloat16
HIGHEST = lax.Precision.HIGHEST

NORM_EPS = 1e-6
RWKV_GN_EPS = 64e-5
LANES = 128
VMEM_LIMIT_BYTES = 48 * 1024 * 1024
NEG_BIG = -1e30

FOX_HEAD_DIM = 64
GDN_HEAD_DIM = 128
GDN_CONV_W = 4
RWKV_HEAD_DIM = 64
CHUNK = 64
FOX_Q_TILE = 256
FOX_K_TILE = 512
DECODE_PAGES_PER_STEP = 8


def _params(*semantics):
    return pltpu.CompilerParams(dimension_semantics=semantics, vmem_limit_bytes=VMEM_LIMIT_BYTES)


def _sigmoid(x):
    return 1.0 / (1.0 + jnp.exp(-x))


def _rms(x, gain):
    return x * lax.rsqrt(jnp.mean(x * x, axis=-1, keepdims=True) + NORM_EPS) * gain


def _dot(a, b):
    return jnp.dot(a, b, preferred_element_type=F32)


def _dot_nt(a, b):
    return lax.dot_general(a, b, (((1,), (1,)), ((), ())), preferred_element_type=F32)


def _dot_tn(a, b):
    return lax.dot_general(a, b, (((0,), (0,)), ((), ())), preferred_element_type=F32)


def _dot_hi(a, b):
    return jnp.dot(a, b, preferred_element_type=F32, precision=HIGHEST)


def _dot_nt_hi(a, b):
    return lax.dot_general(a, b, (((1,), (1,)), ((), ())), preferred_element_type=F32, precision=HIGHEST)


def _dot_tn_hi(a, b):
    return lax.dot_general(a, b, (((0,), (0,)), ((), ())), preferred_element_type=F32, precision=HIGHEST)


def _bf(x):
    return x.astype(BF16)


def _top_bits(x):
    bits = lax.bitcast_convert_type(x, jnp.int32) & jnp.int32(-65536)
    return lax.bitcast_convert_type(bits, F32)


def _split2(x):
    hi = _top_bits(x)
    return _bf(hi), _bf(x - hi)


def _split3(x):
    hi = _top_bits(x)
    r = x - hi
    mid = _top_bits(r)
    return _bf(hi), _bf(mid), _bf(r - mid)


def _dot_x3(a2, b2):
    (ah, al), (bh, bl) = a2, b2
    return _dot(ah, bh) + (_dot(ah, bl) + _dot(al, bh))


def _pad_cols(w, n):
    return jnp.pad(w, ((0, 0), (0, n - w.shape[1])))


def _pad_rows(w, n):
    return jnp.pad(w, ((0, n - w.shape[0]), (0, 0)))


def _col_tile(n, cap=1536):
    best = LANES
    for t in range(LANES, min(n, cap) + 1, LANES):
        if n % t == 0:
            best = t
    return best


def _mm_kernel(*refs, norm, residual):
    it = iter(refs)
    x_ref = next(it)
    g_ref = next(it) if norm else None
    w_ref = next(it)
    r_ref = next(it) if residual else None
    o_ref = next(it)
    xn_ref = next(it) if norm else None
    if norm:
        @pl.when(pl.program_id(1) == 0)
        def _():
            xn_ref[...] = _bf(_rms(x_ref[...], g_ref[...]))
        a = xn_ref[...]
    else:
        a = _bf(x_ref[...])
    acc = _dot(a, w_ref[...])
    if residual:
        acc = acc + r_ref[...]
    o_ref[...] = acc.astype(o_ref.dtype)


def mm(x, w, *, gain=None, res=None, out_dtype=F32):
    m, k = x.shape
    n = w.shape[1]
    tm = min(m, 512)
    tn = _col_tile(n)
    norm, residual = gain is not None, res is not None
    in_specs = [pl.BlockSpec((tm, k), lambda i, j: (i, 0))]
    args = [x]
    if norm:
        in_specs.append(pl.BlockSpec((1, k), lambda i, j: (0, 0)))
        args.append(gain.reshape(1, k))
    in_specs.append(pl.BlockSpec((k, tn), lambda i, j: (0, j)))
    args.append(w)
    if residual:
        in_specs.append(pl.BlockSpec((tm, tn), lambda i, j: (i, j)))
        args.append(res)
    return pl.pallas_call(
        functools.partial(_mm_kernel, norm=norm, residual=residual),
        grid=(m // tm, n // tn),
        in_specs=in_specs,
        out_specs=pl.BlockSpec((tm, tn), lambda i, j: (i, j)),
        out_shape=jax.ShapeDtypeStruct((m, n), out_dtype),
        scratch_shapes=[pltpu.VMEM((tm, k), BF16)] if norm else [],
        compiler_params=_params("parallel", "arbitrary"),
        name="dense",
    )(*args)


def _mm_t_kernel(x_ref, g_ref, w_ref, o_ref, xn_ref):
    @pl.when(pl.program_id(1) == 0)
    def _():
        xn_ref[...] = _bf(_rms(x_ref[...], g_ref[...]))

    o_ref[...] = _dot_nt(w_ref[...], xn_ref[...])


def mm_t(x, w_t, gain, *, batch, seq):
    m, k = x.shape
    n = w_t.shape[0]
    tm = min(seq, 512)
    tn = min(n, 1024)
    nt = seq // tm
    return pl.pallas_call(
        _mm_t_kernel,
        grid=(m // tm, n // tn),
        in_specs=[pl.BlockSpec((tm, k), lambda i, j: (i, 0)),
                  pl.BlockSpec((1, k), lambda i, j: (0, 0)),
                  pl.BlockSpec((tn, k), lambda i, j: (j, 0))],
        out_specs=pl.BlockSpec((None, tn, tm), lambda i, j: (i // nt, j, i % nt)),
        out_shape=jax.ShapeDtypeStruct((batch, n, seq), F32),
        scratch_shapes=[pltpu.VMEM((tm, k), BF16)],
        compiler_params=_params("parallel", "arbitrary"),
        name="dense_t",
    )(x, gain.reshape(1, k), w_t)


def _ffn_kernel(*refs, final):
    if final:
        x_ref, g_ref, wg_ref, wu_ref, wo_ref, fg_ref, o_ref, xn_ref, acc_ref = refs
    else:
        x_ref, g_ref, wg_ref, wu_ref, wo_ref, o_ref, xn_ref, acc_ref = refs
    k = pl.program_id(1)

    @pl.when(k == 0)
    def _():
        xn_ref[...] = _bf(_rms(x_ref[...], g_ref[...]))
        acc_ref[...] = jnp.zeros_like(acc_ref)

    xn = xn_ref[...]
    gate = _dot(xn, wg_ref[...])
    up = _dot(xn, wu_ref[...])
    act = _bf(gate * _sigmoid(gate) * up)
    acc_ref[...] += _dot(act, wo_ref[...])

    @pl.when(k == pl.num_programs(1) - 1)
    def _():
        out = x_ref[...] + acc_ref[...]
        if final:
            out = _rms(out, fg_ref[...])
        o_ref[...] = out


def ffn(x, gain, w_in, w_out, final_gain=None):
    m, d = x.shape
    hidden = w_out.shape[0]
    th = 256
    nk = hidden // th
    tm = min(m, 1024)
    final = final_gain is not None
    in_specs = [
        pl.BlockSpec((tm, d), lambda i, k: (i, 0)),
        pl.BlockSpec((1, d), lambda i, k: (0, 0)),
        pl.BlockSpec((d, th), lambda i, k: (0, k)),
        pl.BlockSpec((d, th), lambda i, k: (0, k + nk)),
        pl.BlockSpec((th, d), lambda i, k: (k, 0)),
    ]
    args = [x, gain.reshape(1, d), w_in, w_in, w_out]
    if final:
        in_specs.append(pl.BlockSpec((1, d), lambda i, k: (0, 0)))
        args.append(final_gain.reshape(1, d))
    return pl.pallas_call(
        functools.partial(_ffn_kernel, final=final),
        grid=(m // tm, nk),
        in_specs=in_specs,
        out_specs=pl.BlockSpec((tm, d), lambda i, k: (i, 0)),
        out_shape=jax.ShapeDtypeStruct((m, d), F32),
        scratch_shapes=[pltpu.VMEM((tm, d), BF16), pltpu.VMEM((tm, d), F32)],
        compiler_params=_params("parallel", "arbitrary"),
        name="swiglu",
    )(*args)


def _fox_flash_kernel(q_ref, k_ref, vt_ref, g_ref, o_ref, m_ref, acc_ref, s_buf, p_buf, a_buf, *, tq, tk):
    i = pl.program_id(2)
    dh = FOX_HEAD_DIM
    heads = range(2)
    n_full = (i * tq) // tk
    m_ref[...] = jnp.full_like(m_ref, NEG_BIG)
    acc_ref[...] = jnp.zeros_like(acc_ref)
    p_buf[...] = jnp.zeros_like(p_buf)
    a_buf[...] = jnp.ones_like(a_buf)
    key = lax.broadcasted_iota(jnp.int32, (tk, tq), 0)
    qry = lax.broadcasted_iota(jnp.int32, (tk, tq), 1)

    def scores(j, slot):
        off = pl.multiple_of(j * tk, tk)
        for hh in heads:
            s_buf[slot, hh] = _dot(k_ref[pl.ds(off, tk), hh * LANES:(hh + 1) * LANES], q_ref[hh])

    def softmax(j, slot, masked):
        ss = [s_buf[slot, hh] for hh in heads]
        if masked:
            ss = [jnp.where(key + j * tk <= qry + i * tq, s, NEG_BIG) for s in ss]
        m_prev = [m_ref[hh] for hh in heads]
        m_new = [jnp.maximum(m_prev[hh], jnp.max(ss[hh], axis=0, keepdims=True)) for hh in heads]
        for hh in heads:
            p_buf[slot, hh] = _bf(jnp.exp(ss[hh] - m_new[hh]))
            a_buf[slot, hh] = jnp.exp(m_prev[hh] - m_new[hh])
            m_ref[hh] = m_new[hh]

    def values(j, slot):
        off = pl.multiple_of(jnp.maximum(j, 0) * tk, tk)
        for hh in heads:
            acc_ref[hh] = a_buf[slot, hh] * acc_ref[hh] + _dot(vt_ref[hh, :, pl.ds(off, tk)], p_buf[slot, hh])

    scores(0, 0)

    def pair(u, carry):
        j = 2 * u
        scores(j + 1, 1)
        softmax(j, 0, False)
        values(j - 1, 1)
        scores(j + 2, 0)
        softmax(j + 1, 1, False)
        values(j, 0)
        return carry

    n_pairs = n_full // 2
    lax.fori_loop(0, n_pairs, pair, 0)
    j = 2 * n_pairs

    @pl.when(n_full - j == 1)
    def _():
        scores(j + 1, 1)
        softmax(j, 0, False)
        values(j - 1, 1)
        softmax(j + 1, 1, True)
        values(j, 0)
        values(j + 1, 1)

    @pl.when(n_full - j == 0)
    def _():
        softmax(j, 0, True)
        values(j - 1, 1)
        values(j, 0)

    halves = []
    for hh in range(2):
        a = acc_ref[hh].T
        halves.append(a[:, :dh] / a[:, dh:dh + 1])
    o_ref[...] = _bf(jnp.concatenate(halves, axis=-1) * _sigmoid(g_ref[...]))


def fox_flash(qt, ka, vt, gate_src, gate_col, *, batch, seq):
    m = ka.shape[0]
    n_pairs = vt.shape[1]
    tq = min(seq, FOX_Q_TILE)
    tk = min(seq, FOX_K_TILE)
    nq = seq // tq
    return pl.pallas_call(
        functools.partial(_fox_flash_kernel, tq=tq, tk=tk),
        grid=(batch, n_pairs, nq),
        in_specs=[
            pl.BlockSpec((None, None, 2, LANES, tq), lambda b, h, i: (b, h, 0, 0, i)),
            pl.BlockSpec((seq, 2 * LANES), lambda b, h, i: (b, h)),
            pl.BlockSpec((None, None, 2, LANES, seq), lambda b, h, i: (b, h, 0, 0, 0)),
            pl.BlockSpec((tq, LANES), lambda b, h, i: (b * nq + i, gate_col + h)),
        ],
        out_specs=pl.BlockSpec((tq, LANES), lambda b, h, i: (b * nq + i, h)),
        out_shape=jax.ShapeDtypeStruct((m, n_pairs * LANES), BF16),
        scratch_shapes=[pltpu.VMEM((2, 1, tq), F32), pltpu.VMEM((2, LANES, tq), F32),
                        pltpu.VMEM((2, 2, tk, tq), F32), pltpu.VMEM((2, 2, tk, tq), BF16),
                        pltpu.VMEM((2, 2, 1, tq), F32)],
        compiler_params=_params("parallel", "parallel", "arbitrary"),
        name="fox_flash",
    )(qt, ka, vt, gate_src)


def _fox_decode_kernel(pt_ref, q_ref, kn_ref, vn_ref, lfn_ref, g_ref, *refs, pps):
    lf_refs, k_refs, v_refs = refs[:pps], refs[pps:2 * pps], refs[2 * pps:3 * pps]
    o_ref, qb_ref, m_ref, l_ref, carry_ref, acc_ref = refs[3 * pps:]
    p = pl.program_id(1)
    n_heads, dh, page = k_refs[0].shape
    heads = range(n_heads)

    @pl.when(p == 0)
    def _():
        eye = (lax.broadcasted_iota(jnp.int32, (dh, dh), 0) == lax.broadcasted_iota(jnp.int32, (dh, dh), 1)).astype(F32)
        q_t = _dot_nt_hi(eye, q_ref[...])
        kn_t = _dot_nt_hi(eye, kn_ref[...])
        vn_t = _dot_nt_hi(eye, vn_ref[...])
        lane0 = lax.broadcasted_iota(jnp.int32, (dh, page), 1) == 0
        for h in heads:
            q_col = q_t[:, h:h + 1]
            qb_ref[h] = jnp.broadcast_to(q_col, (dh, page))
            s_new = jnp.sum(q_col * kn_t[:, h:h + 1], axis=0, keepdims=True)
            m_ref[h:h + 1, :] = jnp.broadcast_to(s_new, (1, page))
            acc_ref[h] = jnp.where(lane0, jnp.broadcast_to(vn_t[:, h:h + 1], (dh, page)), 0.0)
        l_ref[...] = jnp.ones_like(l_ref)
        carry_ref[...] = jnp.broadcast_to(lfn_ref[...], (n_heads, page))

    later = (lax.broadcasted_iota(jnp.int32, (page, page), 0) > lax.broadcasted_iota(jnp.int32, (page, page), 1)).astype(F32)
    carry = carry_ref[...]
    ss = []
    for i in range(pps):
        lf = lf_refs[i][...]
        bias = carry + _dot_hi(lf, later)
        carry = carry + jnp.sum(lf, axis=-1, keepdims=True)
        ss.append(jnp.concatenate([jnp.sum(qb_ref[h] * k_refs[i][h], axis=0, keepdims=True) for h in heads], axis=0)
                  + bias)
    carry_ref[...] = carry
    m_prev = m_ref[...]
    m_new = m_prev
    for s in ss:
        m_new = jnp.maximum(m_new, jnp.max(s, axis=-1, keepdims=True))
    alpha = jnp.exp(m_prev - m_new)
    prs = [jnp.exp(s - m_new) for s in ss]
    l_new = alpha * l_ref[...]
    for pr in prs:
        l_new = l_new + jnp.sum(pr, axis=-1, keepdims=True)
    l_ref[...] = l_new
    m_ref[...] = m_new
    for h in heads:
        acc = alpha[h:h + 1, :] * acc_ref[h]
        for i in range(pps):
            acc = acc + prs[i][h:h + 1, :] * v_refs[i][h]
        acc_ref[h] = acc

    @pl.when(p == pl.num_programs(1) - 1)
    def _():
        for h in heads:
            acc_ref[h] = acc_ref[h] / l_ref[h:h + 1, :]
        ones = jnp.ones((8, page), F32)
        o = _dot_nt_hi(ones, acc_ref[...].reshape(n_heads * dh, page))
        o_ref[...] = o[0:1, :] * g_ref[...]


def fox_decode(layer_idx, page_table, q, k_new, v_new, lf_new, gate, cache_logf_t, cache_k_t, cache_v_t):
    db, n_heads, dh = q.shape
    n_pages = page_table.shape[1]
    page = cache_k_t.shape[4]
    pps = DECODE_PAGES_PER_STEP if n_pages % DECODE_PAGES_PER_STEP == 0 else 1
    vec = pl.BlockSpec((None, n_heads, dh), lambda b, p, pt: (b, 0, 0))
    row = pl.BlockSpec((None, 1, n_heads * dh), lambda b, p, pt: (b, 0, 0))

    def past(i):
        return lambda b, p, pt: pt[b, n_pages - 1 - (p * pps + i)]

    lf_specs = [pl.BlockSpec((None, None, n_heads, page), lambda b, p, pt, f=past(i): (layer_idx, f(b, p, pt), 0, 0))
                for i in range(pps)]
    kv_specs = [pl.BlockSpec((None, None, n_heads, dh, page),
                             lambda b, p, pt, f=past(i): (layer_idx, f(b, p, pt), 0, 0, 0)) for i in range(pps)]
    return pl.pallas_call(
        functools.partial(_fox_decode_kernel, pps=pps),
        grid_spec=pltpu.PrefetchScalarGridSpec(
            num_scalar_prefetch=1,
            grid=(db, n_pages // pps),
            in_specs=[vec, vec, vec, pl.BlockSpec((None, n_heads, 1), lambda b, p, pt: (b, 0, 0)), row]
            + lf_specs + kv_specs + kv_specs,
            out_specs=row,
            scratch_shapes=[pltpu.VMEM((n_heads, dh, page), F32), pltpu.VMEM((n_heads, page), F32),
                            pltpu.VMEM((n_heads, page), F32), pltpu.VMEM((n_heads, page), F32),
                            pltpu.VMEM((n_heads, dh, page), F32)],
        ),
        out_shape=jax.ShapeDtypeStruct((db, 1, n_heads * dh), F32),
        compiler_params=_params("parallel", "arbitrary"),
        name="fox_decode",
    )(page_table, q, k_new, v_new, lf_new, gate, *([cache_logf_t] * pps), *([cache_k_t] * pps), *([cache_v_t] * pps))


def _tri_masks(c):
    row = lax.broadcasted_iota(jnp.int32, (c, c), 0)
    col = lax.broadcasted_iota(jnp.int32, (c, c), 1)
    return row >= col, row > col, row == col


def _cumsum_rows(tri_b, x):
    hi, mid, lo = _split3(x)
    return _dot(tri_b, hi) + (_dot(tri_b, mid) + _dot(tri_b, lo))


def _neumann_inverses(mats, eye, c):
    steps = max(c.bit_length() - 2, 0)
    ps = [eye + a for a in mats]
    if steps == 0:
        return ps
    splits = [_split2(a) for a in mats]
    aks = [_dot_x3(s, s) for s in splits]
    for step in range(steps):
        last = step == steps - 1
        nxt = []
        for p, ak in zip(ps, aks):
            ak2 = _split2(ak)
            lhs = p if last else jnp.concatenate([p, ak], axis=0)
            nxt.append(_dot_x3(_split2(lhs), ak2))
        ps = [p + n[:c] for p, n in zip(ps, nxt)]
        aks = [None if last else n[c:] for n in nxt]
    return ps


def _gdn_chunk_kernel(q_ref, k_ref, v_ref, z_ref, bcol_ref, gcol_ref, grow_ref, gain_ref, o_ref, s_out_ref, s_ref, *, c, hb):
    ch = pl.program_id(2)
    dk = GDN_HEAD_DIM

    @pl.when(ch == 0)
    def _():
        s_ref[...] = jnp.zeros_like(s_ref)

    incl, strict, diag = _tri_masks(c)
    eye = jnp.where(diag, 1.0, 0.0).astype(F32)
    tri_b = jnp.where(incl, 1.0, 0.0).astype(BF16)
    gc_col = _cumsum_rows(tri_b, gcol_ref[...])
    g3 = _split3(grow_ref[...])
    gc_row = _dot_nt(g3[0], tri_b) + (_dot_nt(g3[1], tri_b) + _dot_nt(g3[2], tri_b))
    heads = range(hb)
    vs = [slice(hh * dk, (hh + 1) * dk) for hh in heads]
    q = [q_ref[:, (hh // 2) * dk:(hh // 2 + 1) * dk] for hh in heads]
    k = [k_ref[:, (hh // 2) * dk:(hh // 2 + 1) * dk] for hh in heads]
    beta = [bcol_ref[:, hh:hh + 1] for hh in heads]
    gcc = [gc_col[:, hh:hh + 1] for hh in heads]
    g_last = [g[c - 1:c, :] for g in gcc]
    decay = [jnp.where(incl, jnp.exp(jnp.where(incl, gcc[hh] - gc_row[hh:hh + 1, :], 0.0)), 0.0) for hh in heads]
    kb = [k[hh] * beta[hh] for hh in heads]
    kbf = [_bf(x) for x in k]
    kk = [_dot_nt(_bf(kb[hh]), kbf[hh]) for hh in heads]
    qk = [_dot_nt(_bf(q[hh]), kbf[hh]) for hh in heads]
    neg_lower = [jnp.where(strict, -(kk[hh] * decay[hh]), 0.0) for hh in heads]
    attn = [_bf(jnp.where(incl, qk[hh] * decay[hh], 0.0)) for hh in heads]
    e_gc = [jnp.exp(g) for g in gcc]
    rhs = [_split2(jnp.concatenate([v_ref[:, vs[hh]] * beta[hh], kb[hh] * e_gc[hh]], axis=-1)) for hh in heads]
    q_dec = [_bf(q[hh] * e_gc[hh]) for hh in heads]
    k_dec = [_bf(k[hh] * jnp.exp(g_last[hh] - gcc[hh])) for hh in heads]
    t_inv = _neumann_inverses(neg_lower, eye, c)
    sol = [_dot_x3(_split2(t_inv[hh]), rhs[hh]) for hh in heads]
    s = [s_ref[hh] for hh in heads]
    sb = [_bf(x) for x in s]
    vnb = [_bf(sol[hh][:, :dk] - _dot(_bf(sol[hh][:, dk:]), sb[hh])) for hh in heads]
    o = [_dot(q_dec[hh], sb[hh]) + _dot(attn[hh], vnb[hh]) for hh in heads]
    s_new = [s[hh] * jnp.exp(g_last[hh]) + _dot_tn(k_dec[hh], vnb[hh]) for hh in heads]
    for hh in heads:
        s_ref[hh] = s_new[hh]
        z = z_ref[:, vs[hh]]
        o_ref[:, vs[hh]] = _bf(_rms(o[hh], gain_ref[...]) * (z * _sigmoid(z)))

    @pl.when(ch == pl.num_programs(2) - 1)
    def _():
        s_out_ref[...] = s_ref[...]


def gdn_chunk_scan(q, k, v, z, beta, g, gain, *, batch, seq):
    m = q.shape[0]
    n_heads = v.shape[1] // GDN_HEAD_DIM
    c = min(CHUNK, seq)
    hb = min(16, n_heads)
    n_groups = n_heads // hb
    n = seq // c
    bcol = beta.reshape(m, n_groups, hb).transpose(1, 0, 2)
    gcol = g.reshape(m, n_groups, hb).transpose(1, 0, 2)
    grow = g.reshape(batch * n, c, n_groups, hb).transpose(0, 2, 3, 1)
    qk_w = (hb // 2) * GDN_HEAD_DIM
    v_w = hb * GDN_HEAD_DIM
    row_blk = lambda b, h, t: (b * n + t, h)
    col_spec = pl.BlockSpec((None, c, hb), lambda b, h, t: (h, b * n + t, 0))
    return pl.pallas_call(
        functools.partial(_gdn_chunk_kernel, c=c, hb=hb),
        grid=(batch, n_groups, n),
        in_specs=[
            pl.BlockSpec((c, qk_w), row_blk),
            pl.BlockSpec((c, qk_w), row_blk),
            pl.BlockSpec((c, v_w), row_blk),
            pl.BlockSpec((c, v_w), row_blk),
            col_spec, col_spec,
            pl.BlockSpec((None, None, hb, c), lambda b, h, t: (b * n + t, h, 0, 0)),
            pl.BlockSpec((1, GDN_HEAD_DIM), lambda b, h, t: (0, 0)),
        ],
        out_specs=[
            pl.BlockSpec((c, v_w), row_blk),
            pl.BlockSpec((None, hb, GDN_HEAD_DIM, GDN_HEAD_DIM), lambda b, h, t: (b, h, 0, 0)),
        ],
        out_shape=[jax.ShapeDtypeStruct((m, n_heads * GDN_HEAD_DIM), BF16),
                   jax.ShapeDtypeStruct((batch, n_heads, GDN_HEAD_DIM, GDN_HEAD_DIM), F32)],
        scratch_shapes=[pltpu.VMEM((hb, GDN_HEAD_DIM, GDN_HEAD_DIM), F32)],
        compiler_params=_params("parallel", "parallel", "arbitrary"),
        name="gdn_chunk_scan",
    )(q, k, v, z, bcol, gcol, grow, gain.reshape(1, GDN_HEAD_DIM))


def _gdn_step_kernel(q_ref, k_ref, v_ref, z_ref, beta_ref, g_ref, gain_ref, s0_ref, o_ref, s_ref):
    dk = GDN_HEAD_DIM
    n_heads = s0_ref.shape[0]
    n_qk = q_ref.shape[0]
    rep = n_heads // n_qk
    q_rows = q_ref[...]
    k_rows = k_ref[...]
    eye = (lax.broadcasted_iota(jnp.int32, (dk, dk), 0) == lax.broadcasted_iota(jnp.int32, (dk, dk), 1)).astype(F32)
    cols = _dot_nt_hi(eye, jnp.concatenate([q_rows, k_rows], axis=0))
    qk_dot = jnp.sum(q_rows * k_rows, axis=-1, keepdims=True)
    for j in range(n_qk):
        q_c = jnp.broadcast_to(cols[:, j:j + 1], (dk, dk))
        k_c = jnp.broadcast_to(cols[:, n_qk + j:n_qk + j + 1], (dk, dk))
        for h in range(j * rep, (j + 1) * rep):
            vs = slice(h * dk, (h + 1) * dk)
            beta = beta_ref[:, h:h + 1]
            e_g = jnp.exp(g_ref[:, h:h + 1])
            s0 = s0_ref[h]
            w_s = jnp.sum(s0 * k_c, axis=0, keepdims=True) * (beta * e_g)
            q_s = jnp.sum(s0 * q_c, axis=0, keepdims=True) * e_g
            v_new = v_ref[:, vs] * beta - w_s
            o = q_s + qk_dot[j:j + 1, :] * v_new
            s_ref[h] = s0 * e_g + k_c * v_new
            z = z_ref[:, vs]
            o_ref[:, vs] = _rms(o, gain_ref[...]) * (z * _sigmoid(z))


def gdn_step(q, k, v, z, beta, g, gain, s0):
    db, n_qk, dk = q.shape
    n_heads = s0.shape[1]
    vec = lambda w: pl.BlockSpec((None, 1, w), lambda b: (b, 0, 0))
    rows = pl.BlockSpec((None, n_qk, dk), lambda b: (b, 0, 0))
    st = pl.BlockSpec((None, n_heads, dk, dk), lambda b: (b, 0, 0, 0))
    return pl.pallas_call(
        _gdn_step_kernel,
        grid=(db,),
        in_specs=[rows, rows, vec(v.shape[2]), vec(v.shape[2]), vec(n_heads), vec(n_heads),
                  pl.BlockSpec((1, dk), lambda b: (0, 0)), st],
        out_specs=[vec(v.shape[2]), st],
        out_shape=[jax.ShapeDtypeStruct(v.shape, F32), jax.ShapeDtypeStruct(s0.shape, F32)],
        compiler_params=_params("parallel"),
        name="gdn_step",
    )(q, k, v, z, beta, g, gain.reshape(1, dk), s0)


def _rwkv_post(y, r, k, v, gate, rk, lnw, lnb):
    mean = jnp.mean(y, axis=-1, keepdims=True)
    var = jnp.mean(jnp.square(y - mean), axis=-1, keepdims=True)
    y = (y - mean) * lax.rsqrt(var + RWKV_GN_EPS) * lnw + lnb
    y = y + jnp.sum(r * k * rk, axis=-1, keepdims=True) * v
    return y * gate


def _rwkv_keys(k, a, kk_w, ka_w):
    kk = k * kk_w
    kk = kk * lax.rsqrt(jnp.sum(kk * kk, axis=-1, keepdims=True) + 1e-6)
    k = k * (1.0 + (a - 1.0) * ka_w)
    return k, -kk, kk * a


def _rwkv_chunk_kernel(r_ref, lw_ref, k_ref, v_ref, a_ref, gate_ref, kk_ref, ka_ref, rk_ref, lnw_ref, lnb_ref,
                       o_ref, s_out_ref, s_ref, *, c, hb):
    ch = pl.program_id(2)
    n = RWKV_HEAD_DIM

    @pl.when(ch == 0)
    def _():
        s_ref[...] = jnp.zeros_like(s_ref)

    incl, strict, diag = _tri_masks(c)
    eye = jnp.where(diag, 1.0, 0.0).astype(F32)
    tri_b = jnp.where(incl, 1.0, 0.0).astype(BF16)
    gi_all = _cumsum_rows(tri_b, lw_ref[...])
    heads = range(hb)
    sl = [slice(hh * n, (hh + 1) * n) for hh in heads]
    r = [r_ref[:, s_] for s_ in sl]
    v = [v_ref[:, s_] for s_ in sl]
    vb = [_bf(x) for x in v]
    keys = [_rwkv_keys(k_ref[:, s_], a_ref[:, s_], kk_ref[:, s_], ka_ref[:, s_]) for s_ in sl]
    k = [t[0] for t in keys]
    a_vec = [t[1] for t in keys]
    b_vec = [t[2] for t in keys]
    gi = [gi_all[:, s_] for s_ in sl]
    g_last = [g[c - 1:c, :] for g in gi]
    e_neg = [jnp.exp(-g) for g in gi]
    e_rest = [jnp.exp(g_last[hh] - gi[hh]) for hh in heads]
    a_t = [a_vec[hh] * jnp.exp(gi[hh] - lw_ref[:, sl[hh]]) for hh in heads]
    r_tb = [_bf(r[hh] * jnp.exp(gi[hh])) for hh in heads]
    lhs = [jnp.concatenate([_bf(a_t[hh]), r_tb[hh]], axis=0) for hh in heads]
    rhs = [_bf(jnp.concatenate([b_vec[hh] * e_neg[hh], k[hh] * e_neg[hh]], axis=0)) for hh in heads]
    tail = [_bf(jnp.concatenate([b_vec[hh] * e_rest[hh], k[hh] * e_rest[hh]], axis=0)) for hh in heads]
    m4 = [_dot_nt(lhs[hh], rhs[hh]) for hh in heads]
    a_ab = [jnp.where(strict, m[:c, :c], 0.0) for m in m4]
    a_ak = [_bf(jnp.where(strict, m[:c, c:], 0.0)) for m in m4]
    a_rb = [_bf(jnp.where(incl, m[c:, :c], 0.0)) for m in m4]
    a_rk = [_bf(jnp.where(incl, m[c:, c:], 0.0)) for m in m4]
    akv = [_dot(a_ak[hh], vb[hh]) for hh in heads]
    t_inv = _neumann_inverses(a_ab, eye, c)
    sol = [_dot_x3(_split2(t_inv[hh]), _split2(jnp.concatenate([a_t[hh], akv[hh]], axis=-1))) for hh in heads]
    s = [s_ref[hh] for hh in heads]
    sb = [_bf(x) for x in s]
    ub = [_bf(_dot_nt(_bf(sol[hh][:, :n]), sb[hh]) + sol[hh][:, n:]) for hh in heads]
    y = [_dot_nt(r_tb[hh], sb[hh]) + _dot(a_rb[hh], ub[hh]) + _dot(a_rk[hh], vb[hh]) for hh in heads]
    s_new = [s[hh] * jnp.exp(g_last[hh]) + _dot_tn(jnp.concatenate([ub[hh], vb[hh]], axis=0), tail[hh])
             for hh in heads]
    for hh in heads:
        s_ref[hh] = s_new[hh]
        o_ref[:, sl[hh]] = _bf(_rwkv_post(y[hh], r[hh], k[hh], v[hh], gate_ref[:, sl[hh]], rk_ref[:, sl[hh]],
                                          lnw_ref[:, sl[hh]], lnb_ref[:, sl[hh]]))

    @pl.when(ch == pl.num_programs(2) - 1)
    def _():
        s_out_ref[...] = s_ref[...]


def rwkv_chunk_scan(r, lw, k, v, a, gate, kk_w, ka_w, rk, lnw, lnb, *, batch, seq):
    m, d = r.shape
    n_heads = d // RWKV_HEAD_DIM
    hb = min(16, n_heads)
    w = hb * RWKV_HEAD_DIM
    c = min(CHUNK, seq)
    n = seq // c
    tok = pl.BlockSpec((c, w), lambda b, h, t: (b * n + t, h))
    par = pl.BlockSpec((1, w), lambda b, h, t: (0, h))
    return pl.pallas_call(
        functools.partial(_rwkv_chunk_kernel, c=c, hb=hb),
        grid=(batch, n_heads // hb, n),
        in_specs=[tok] * 6 + [par] * 5,
        out_specs=[tok, pl.BlockSpec((None, hb, RWKV_HEAD_DIM, RWKV_HEAD_DIM), lambda b, h, t: (b, h, 0, 0))],
        out_shape=[jax.ShapeDtypeStruct((m, d), BF16),
                   jax.ShapeDtypeStruct((batch, n_heads, RWKV_HEAD_DIM, RWKV_HEAD_DIM), F32)],
        scratch_shapes=[pltpu.VMEM((hb, RWKV_HEAD_DIM, RWKV_HEAD_DIM), F32)],
        compiler_params=_params("parallel", "parallel", "arbitrary"),
        name="rwkv_chunk_scan",
    )(r, lw, k, v, a, gate, kk_w, ka_w, rk, lnw, lnb)


def _rwkv_step_kernel(r_ref, lw_ref, k_ref, v_ref, a_ref, gate_ref, kk_ref, ka_ref, rk_ref, lnw_ref, lnb_ref,
                      s0_ref, o_ref, s_ref, y_ref):
    n = s0_ref.shape[0]
    r = r_ref[...]
    v = v_ref[...]
    k0 = k_ref[...]
    a = a_ref[...]
    kk = k0 * kk_ref[...]
    kk = kk * lax.rsqrt(jnp.sum(kk * kk, axis=0, keepdims=True) + 1e-6)
    k = k0 * (1.0 + (a - 1.0) * ka_ref[...])
    a_vec = -kk
    b_vec = kk * a
    w = jnp.exp(lw_ref[...])

    def body(i, carry):
        s0 = s0_ref[i]
        sa = jnp.sum(s0 * a_vec, axis=0, keepdims=True)
        s = s0 * w + sa * b_vec + v_ref[pl.ds(i, 1), :] * k
        s_ref[i] = s
        y_ref[pl.ds(i, 1), :] = jnp.sum(s * r, axis=0, keepdims=True)
        return carry

    lax.fori_loop(0, n, body, 0, unroll=8)
    y = y_ref[...]
    mean = jnp.mean(y, axis=0, keepdims=True)
    var = jnp.mean(jnp.square(y - mean), axis=0, keepdims=True)
    y = (y - mean) * lax.rsqrt(var + RWKV_GN_EPS) * lnw_ref[...] + lnb_ref[...]
    y = y + jnp.sum(r * k * rk_ref[...], axis=0, keepdims=True) * v
    o_ref[...] = y * gate_ref[...]


def rwkv_step(r, lw, k, v, a, gate, kk_w, ka_w, rk, lnw, lnb, s0_t):
    d, db = r.shape
    n = RWKV_HEAD_DIM
    n_heads = d // n
    tok = pl.BlockSpec((n, db), lambda h: (h, 0))
    par = pl.BlockSpec((n, 1), lambda h: (h, 0))
    st = pl.BlockSpec((None, n, n, db), lambda h: (h, 0, 0, 0))
    return pl.pallas_call(
        _rwkv_step_kernel,
        grid=(n_heads,),
        in_specs=[tok] * 6 + [par] * 5 + [st],
        out_specs=[tok, st],
        out_shape=[jax.ShapeDtypeStruct((d, db), F32), jax.ShapeDtypeStruct(s0_t.shape, F32)],
        scratch_shapes=[pltpu.VMEM((n, db), F32)],
        compiler_params=_params("parallel"),
        name="rwkv_step",
    )(r, lw, k, v, a, gate, kk_w, ka_w, rk, lnw, lnb, s0_t)


def _head_rms(x, gain, n_heads):
    m = x.shape[0]
    xh = x.reshape(m, n_heads, -1)
    return (xh * lax.rsqrt(jnp.mean(xh * xh, axis=-1, keepdims=True) + NORM_EPS) * gain).reshape(m, -1)


def _fox_project(x, gain, w_in, b_f, q_gain, k_gain):
    width = (w_in.shape[1] // LANES) * LANES
    n_heads = w_in.shape[1] - width
    w = _bf(jnp.concatenate([w_in[:, :width], _pad_cols(w_in[:, width:], LANES)], axis=1))
    h = mm(x, w, gain=gain)
    d = width // 4
    q = _head_rms(h[:, :d], q_gain, n_heads) * (FOX_HEAD_DIM ** -0.5)
    k = _head_rms(h[:, d:2 * d], k_gain, n_heads)
    v = h[:, 2 * d:3 * d]
    gate = _sigmoid(h[:, 3 * d:4 * d])
    log_f = jax.nn.log_sigmoid(h[:, width:width + n_heads] + b_f)
    return q, k, v, gate, log_f


def _fox_prompt(x, gain, w_in, b_f, q_gain, k_gain, batch, seq):
    m = batch * seq
    dh = FOX_HEAD_DIM
    width = (w_in.shape[1] // LANES) * LANES
    n_heads = w_in.shape[1] - width
    d = width // 4
    qv_t = mm_t(x, _bf(jnp.concatenate([w_in[:, :d], w_in[:, 2 * d:3 * d]], axis=1).T), gain, batch=batch, seq=seq)
    rest = jnp.concatenate([w_in[:, d:2 * d], w_in[:, 3 * d:4 * d], _pad_cols(w_in[:, width:], 2 * LANES)], axis=1)
    h = mm(x, _bf(rest), gain=gain)
    k = _head_rms(h[:, :d], k_gain, n_heads)
    lf = jax.nn.log_sigmoid(h[:, 2 * d:2 * d + n_heads] + b_f)
    cum = jnp.cumsum(lf.reshape(batch, seq, n_heads), axis=1).reshape(m, n_heads)
    q_t = qv_t[:, :d].reshape(batch, n_heads, dh, seq)
    q_t = q_t * lax.rsqrt(jnp.mean(q_t * q_t, axis=2, keepdims=True) + NORM_EPS) * (q_gain * dh ** -0.5)[:, None]
    v_t = qv_t[:, d:].reshape(batch, n_heads, dh, seq)
    rows = lambda n, val: jnp.full((batch, n_heads, n, seq), val, BF16)
    qt = jnp.concatenate([_bf(q_t), rows(3, 1.0), rows(LANES - dh - 3, 0.0)], axis=2)
    vt = jnp.concatenate([_bf(v_t), rows(1, 1.0), rows(LANES - dh - 1, 0.0)], axis=2)
    neg_c3 = jnp.stack(_split3(-cum), axis=-1)
    ka = jnp.concatenate([_bf(k).reshape(m, n_heads, dh), neg_c3, jnp.zeros((m, n_heads, LANES - dh - 3), BF16)],
                         axis=-1).reshape(m, n_heads * LANES)
    pair = lambda a: a.reshape(batch, n_heads // 2, 2, LANES, seq)
    return pair(qt), ka, pair(vt), h, d // LANES, k, v_t.transpose(0, 3, 1, 2), lf


def fox_layer(hp, hs, gain, w_in, b_f, q_gain, k_gain, w_out, layer_idx, cache_k_t, cache_v_t, cache_logf_t,
              page_table, batch, seq):
    n_heads = b_f.shape[0]
    dh = FOX_HEAD_DIM
    w_out_b = _bf(w_out)
    qt, ka, vt, gate_src, gate_col, k, v, lf = _fox_prompt(hp, gain, w_in, b_f, q_gain, k_gain, batch, seq)
    og = fox_flash(qt, ka, vt, gate_src, gate_col, batch=batch, seq=seq)
    hp = mm(og, w_out_b, res=hp)
    outs_p = (k.reshape(batch, seq, n_heads, dh), v, lf.reshape(batch, seq, n_heads))
    db = hs.shape[0]
    q, k, v, gate, lf = _fox_project(hs, gain, w_in, b_f, q_gain, k_gain)
    og = fox_decode(layer_idx, page_table, q.reshape(db, n_heads, dh), k.reshape(db, n_heads, dh),
                    v.reshape(db, n_heads, dh), lf.reshape(db, n_heads, 1), gate.reshape(db, 1, n_heads * dh),
                    cache_logf_t, cache_k_t, cache_v_t)
    hs = mm(og.reshape(db, n_heads * dh), w_out_b, res=hs)
    outs_s = (k.reshape(db, 1, n_heads, dh), v.reshape(db, 1, n_heads, dh), lf.reshape(db, 1, n_heads))
    return hp, hs, outs_p, outs_s


def _gdn_project(x, gain, w_in, n_heads):
    total = w_in.shape[1]
    main = total - 2 * n_heads
    w = _bf(jnp.concatenate([w_in[:, :main], _pad_cols(w_in[:, main:], LANES)], axis=1))
    h = mm(x, w, gain=gain)
    conv_ch = main - n_heads * GDN_HEAD_DIM
    return h[:, :conv_ch], h[:, conv_ch:main], h[:, main:main + n_heads], h[:, main + n_heads:main + 2 * n_heads]


def _gdn_post_conv(conv, a, beta_logit, a_log, dt_bias, n_heads):
    m = conv.shape[0]
    conv = conv * _sigmoid(conv)
    qk_w = (n_heads // 2) * GDN_HEAD_DIM

    def l2(x):
        xh = x.reshape(m, n_heads // 2, GDN_HEAD_DIM)
        return (xh * lax.rsqrt(jnp.sum(xh * xh, axis=-1, keepdims=True) + 1e-6)).reshape(m, qk_w)

    q = l2(conv[:, :qk_w]) * (GDN_HEAD_DIM ** -0.5)
    k = l2(conv[:, qk_w:2 * qk_w])
    v = conv[:, 2 * qk_w:]
    beta = _sigmoid(beta_logit)
    g = -jnp.exp(a_log) * jax.nn.softplus(a + dt_bias)
    return q, k, v, beta, g


def gdn_layer(hp, hs, gain, w_in, conv_w, a_log, dt_bias, out_gain, w_out, conv_state, s_state, batch, seq):
    n_heads = a_log.shape[0]
    w_out_b = _bf(w_out)
    qkv, z, a, bl = _gdn_project(hp, gain, w_in, n_heads)
    ch = qkv.shape[1]
    xc = jnp.concatenate([jnp.zeros((batch, GDN_CONV_W - 1, ch), F32), qkv.reshape(batch, seq, ch)], axis=1)
    conv = xc[:, 0:seq] * conv_w[0]
    for i in range(1, GDN_CONV_W):
        conv = conv + xc[:, i:i + seq] * conv_w[i]
    q, k, v, beta, g = _gdn_post_conv(conv.reshape(batch * seq, ch), a, bl, a_log, dt_bias, n_heads)
    og, s_p = gdn_chunk_scan(q, k, v, z, beta, g, out_gain, batch=batch, seq=seq)
    hp = mm(og, w_out_b, res=hp)
    conv_p = xc[:, seq:]
    db = hs.shape[0]
    qkv, z, a, bl = _gdn_project(hs, gain, w_in, n_heads)
    xc = jnp.concatenate([conv_state, qkv[:, None, :]], axis=1)
    conv = xc[:, 0] * conv_w[0]
    for i in range(1, GDN_CONV_W):
        conv = conv + xc[:, i] * conv_w[i]
    q, k, v, beta, g = _gdn_post_conv(conv, a, bl, a_log, dt_bias, n_heads)
    og, s_s = gdn_step(q.reshape(db, -1, GDN_HEAD_DIM), k.reshape(db, -1, GDN_HEAD_DIM), v[:, None], z[:, None],
                       beta[:, None], g[:, None], out_gain, s_state)
    hs = mm(og.reshape(db, -1), w_out_b, res=hs)
    conv_s = xc[:, 1:]
    return hp, hs, (s_p, conv_p), (s_s, conv_s)


def _rwkv_project(u, u_prev, mu, w0, w1, w2, a0, a1, a2, g1, g2, w_r, w_k, w_v):
    du = u_prev - u
    xr, xw, xk, xv, xa, xg = (u + du * mu[i] for i in range(6))
    lora = lambda w: -(-w.shape[1] // LANES) * LANES
    r = mm(xr, _bf(w_r))
    k = mm(xk, _bf(w_k))
    v = mm(xv, _bf(w_v))
    nw, na, ng = lora(w1), lora(a1), lora(g1)
    w_l = mm(jnp.tanh(mm(xw, _bf(_pad_cols(w1, nw)))), _bf(_pad_rows(w2, nw)))
    a_l = mm(mm(xa, _bf(_pad_cols(a1, na))), _bf(_pad_rows(a2, na)))
    gate = mm(_sigmoid(mm(xg, _bf(_pad_cols(g1, ng)))), _bf(_pad_rows(g2, ng)))
    w_raw = -jax.nn.softplus(-(w0 + w_l)) - 0.5
    log_decay = -jnp.exp(w_raw)
    a = _sigmoid(a0 + a_l)
    return r, log_decay, k, v, a, gate


def rwkv_layer(hp, hs, gain, mu, w0, w1, w2, a0, a1, a2, g1, g2, k_k, k_a, r_k, w_r, w_k, w_v, w_o, ln_w, ln_b,
               shift_state, wkv_state, batch, seq):
    d = hp.shape[1]
    w_o_b = _bf(w_o)
    proj_w = (mu, w0, w1, w2, a0, a1, a2, g1, g2, w_r, w_k, w_v)
    chan = (k_k, k_a, r_k.reshape(d), ln_w, ln_b)
    u = _rms(hp, gain).reshape(batch, seq, d)
    u_prev = jnp.concatenate([jnp.zeros((batch, 1, d), F32), u[:, :-1]], axis=1)
    toks = _rwkv_project(u.reshape(batch * seq, d), u_prev.reshape(batch * seq, d), *proj_w)
    og, s_p = rwkv_chunk_scan(*toks, *(p.reshape(1, d) for p in chan), batch=batch, seq=seq)
    hp = mm(og, w_o_b, res=hp)
    shift_p = u[:, -1]
    us = _rms(hs, gain)
    toks = _rwkv_project(us, shift_state, *proj_w)
    og_t, s_t = rwkv_step(*(t.T for t in toks), *(p.reshape(d, 1) for p in chan), wkv_state.transpose(1, 2, 3, 0))
    hs = mm(og_t.T, w_o_b, res=hs)
    return hp, hs, (shift_p, s_p), (us, s_t.transpose(3, 0, 1, 2))


def kernel(x_prompt, x_sample, cache_k, cache_v, cache_logf, page_table, state_gdn_s, state_gdn_conv, state_rwkv_shift, state_rwkv_wkv, norm_mix, norm_ffn, norm_final, fox_w_in, fox_b_f, fox_q_norm, fox_k_norm, fox_w_out, gdn_w_in, gdn_conv_w, gdn_a_log, gdn_dt_bias, gdn_out_norm, gdn_w_out, rwkv_mu, rwkv_w0, rwkv_w1, rwkv_w2, rwkv_a0, rwkv_a1, rwkv_a2, rwkv_g1, rwkv_g2, rwkv_k_k, rwkv_k_a, rwkv_r_k, rwkv_w_r, rwkv_w_k, rwkv_w_v, rwkv_w_o, rwkv_ln_w, rwkv_ln_b, ffn_w_in, ffn_w_out):
    batch, seq, d = x_prompt.shape
    db = x_sample.shape[0]
    depth = norm_mix.shape[0]
    hp = x_prompt.reshape(batch * seq, d)
    hs = x_sample.reshape(db, d)
    cache_k_t = cache_k.transpose(0, 1, 3, 4, 2)
    cache_v_t = cache_v.transpose(0, 1, 3, 4, 2)
    cache_logf_t = cache_logf.transpose(0, 1, 3, 2)
    fox_p, fox_s, gdn_p, gdn_s, rwkv_p, rwkv_s = [], [], [], [], [], []
    for layer in range(depth):
        kind, j = layer % 3, layer // 3
        if kind == 0:
            hp, hs, op, os_ = fox_layer(hp, hs, norm_mix[layer], fox_w_in[j], fox_b_f[j], fox_q_norm[j], fox_k_norm[j],
                                        fox_w_out[j], j, cache_k_t, cache_v_t, cache_logf_t, page_table, batch, seq)
            fox_p.append(op)
            fox_s.append(os_)
        elif kind == 1:
            hp, hs, op, os_ = gdn_layer(hp, hs, norm_mix[layer], gdn_w_in[j], gdn_conv_w[j], gdn_a_log[j],
                                        gdn_dt_bias[j], gdn_out_norm[j], gdn_w_out[j], state_gdn_conv[j],
                                        state_gdn_s[j], batch, seq)
            gdn_p.append(op)
            gdn_s.append(os_)
        else:
            hp, hs, op, os_ = rwkv_layer(hp, hs, norm_mix[layer], rwkv_mu[j], rwkv_w0[j], rwkv_w1[j], rwkv_w2[j],
                                         rwkv_a0[j], rwkv_a1[j], rwkv_a2[j], rwkv_g1[j], rwkv_g2[j], rwkv_k_k[j],
                                         rwkv_k_a[j], rwkv_r_k[j], rwkv_w_r[j], rwkv_w_k[j], rwkv_w_v[j],
                                         rwkv_w_o[j], rwkv_ln_w[j], rwkv_ln_b[j], state_rwkv_shift[j],
                                         state_rwkv_wkv[j], batch, seq)
            rwkv_p.append(op)
            rwkv_s.append(os_)
        final = norm_final if layer == depth - 1 else None
        w_in_b, w_out_b = _bf(ffn_w_in[layer]), _bf(ffn_w_out[layer])
        hp = ffn(hp, norm_ffn[layer], w_in_b, w_out_b, final)
        hs = ffn(hs, norm_ffn[layer], w_in_b, w_out_b, final)
    stack = lambda items, i: jnp.stack([it[i] for it in items])
    return (hp.reshape(batch, seq, d), hs.reshape(db, 1, d),
            stack(fox_p, 0), stack(fox_p, 1), stack(fox_p, 2),
            stack(fox_s, 0), stack(fox_s, 1), stack(fox_s, 2),
            stack(gdn_p, 0), stack(gdn_p, 1), stack(gdn_s, 0), stack(gdn_s, 1),
            stack(rwkv_p, 0), stack(rwkv_p, 1), stack(rwkv_s, 0), stack(rwkv_s, 1))
```

```python
import functools

import jax
import jax.numpy as jnp
from jax import lax
from jax.experimental import pallas as pl
from jax.experimental.pallas import tpu as pltpu

F32 = jnp.float32
BF16 = jnp.bfloat16
HIGHEST = lax.Precision.HIGHEST

NORM_EPS = 1e-6
RWKV_GN_EPS = 64e-5
LANES = 128
VMEM_LIMIT_BYTES = 48 * 1024 * 1024
NEG_BIG = -1e30

FOX_HEAD_DIM = 64
GDN_HEAD_DIM = 128
GDN_CONV_W = 4
RWKV_HEAD_DIM = 64
CHUNK = 64
FOX_Q_TILE = 512
FOX_K_TILE = 512
DECODE_PAGES_PER_STEP = 8


def _params(*semantics):
    return pltpu.CompilerParams(dimension_semantics=semantics, vmem_limit_bytes=VMEM_LIMIT_BYTES)


def _sigmoid(x):
    return 1.0 / (1.0 + jnp.exp(-x))


def _rms(x, gain):
    return x * lax.rsqrt(jnp.mean(x * x, axis=-1, keepdims=True) + NORM_EPS) * gain


def _dot(a, b):
    return jnp.dot(a, b, preferred_element_type=F32)


def _dot_nt(a, b):
    return lax.dot_general(a, b, (((1,), (1,)), ((), ())), preferred_element_type=F32)


def _dot_tn(a, b):
    return lax.dot_general(a, b, (((0,), (0,)), ((), ())), preferred_element_type=F32)


def _dot_hi(a, b):
    return jnp.dot(a, b, preferred_element_type=F32, precision=HIGHEST)


def _dot_nt_hi(a, b):
    return lax.dot_general(a, b, (((1,), (1,)), ((), ())), preferred_element_type=F32, precision=HIGHEST)


def _dot_tn_hi(a, b):
    return lax.dot_general(a, b, (((0,), (0,)), ((), ())), preferred_element_type=F32, precision=HIGHEST)


def _bf(x):
    return x.astype(BF16)


def _top_bits(x):
    bits = lax.bitcast_convert_type(x, jnp.int32) & jnp.int32(-65536)
    return lax.bitcast_convert_type(bits, F32)


def _split2(x):
    hi = _top_bits(x)
    return _bf(hi), _bf(x - hi)


def _split3(x):
    hi = _top_bits(x)
    r = x - hi
    mid = _top_bits(r)
    return _bf(hi), _bf(mid), _bf(r - mid)


def _dot_x3(a2, b2):
    (ah, al), (bh, bl) = a2, b2
    return _dot(ah, bh) + (_dot(ah, bl) + _dot(al, bh))


def _pad_cols(w, n):
    return jnp.pad(w, ((0, 0), (0, n - w.shape[1])))


def _pad_rows(w, n):
    return jnp.pad(w, ((0, n - w.shape[0]), (0, 0)))


def _col_tile(n, cap=1536):
    best = LANES
    for t in range(LANES, min(n, cap) + 1, LANES):
        if n % t == 0:
            best = t
    return best


def _mm_kernel(*refs, norm, residual):
    it = iter(refs)
    x_ref = next(it)
    g_ref = next(it) if norm else None
    w_ref = next(it)
    r_ref = next(it) if residual else None
    o_ref = next(it)
    xn_ref = next(it) if norm else None
    if norm:
        @pl.when(pl.program_id(1) == 0)
        def _():
            xn_ref[...] = _bf(_rms(x_ref[...], g_ref[...]))
        a = xn_ref[...]
    else:
        a = _bf(x_ref[...])
    acc = _dot(a, w_ref[...])
    if residual:
        acc = acc + r_ref[...]
    o_ref[...] = acc.astype(o_ref.dtype)


def mm(x, w, *, gain=None, res=None, out_dtype=F32):
    m, k = x.shape
    n = w.shape[1]
    tm = min(m, 1024)
    tn = _col_tile(n)
    norm, residual = gain is not None, res is not None
    in_specs = [pl.BlockSpec((tm, k), lambda i, j: (i, 0))]
    args = [x]
    if norm:
        in_specs.append(pl.BlockSpec((1, k), lambda i, j: (0, 0)))
        args.append(gain.reshape(1, k))
    in_specs.append(pl.BlockSpec((k, tn), lambda i, j: (0, j)))
    args.append(w)
    if residual:
        in_specs.append(pl.BlockSpec((tm, tn), lambda i, j: (i, j)))
        args.append(res)
    return pl.pallas_call(
        functools.partial(_mm_kernel, norm=norm, residual=residual),
        grid=(m // tm, n // tn),
        in_specs=in_specs,
        out_specs=pl.BlockSpec((tm, tn), lambda i, j: (i, j)),
        out_shape=jax.ShapeDtypeStruct((m, n), out_dtype),
        scratch_shapes=[pltpu.VMEM((tm, k), BF16)] if norm else [],
        compiler_params=_params("parallel", "arbitrary"),
        name="dense",
    )(*args)


def _mm_t_kernel(x_ref, g_ref, w_ref, o_ref, xn_ref):
    @pl.when(pl.program_id(1) == 0)
    def _():
        xn_ref[...] = _bf(_rms(x_ref[...], g_ref[...]))

    o_ref[...] = _dot_nt(w_ref[...], xn_ref[...])


def mm_t(x, w_t, gain, *, batch, seq):
    m, k = x.shape
    n = w_t.shape[0]
    tm = min(seq, 512)
    tn = min(n, 1024)
    nt = seq // tm
    return pl.pallas_call(
        _mm_t_kernel,
        grid=(m // tm, n // tn),
        in_specs=[pl.BlockSpec((tm, k), lambda i, j: (i, 0)),
                  pl.BlockSpec((1, k), lambda i, j: (0, 0)),
                  pl.BlockSpec((tn, k), lambda i, j: (j, 0))],
        out_specs=pl.BlockSpec((None, tn, tm), lambda i, j: (i // nt, j, i % nt)),
        out_shape=jax.ShapeDtypeStruct((batch, n, seq), F32),
        scratch_shapes=[pltpu.VMEM((tm, k), BF16)],
        compiler_params=_params("parallel", "arbitrary"),
        name="dense_t",
    )(x, gain.reshape(1, k), w_t)


def _ffn_kernel(*refs, final):
    if final:
        x_ref, g_ref, wg_ref, wu_ref, wo_ref, fg_ref, o_ref, xn_ref, acc_ref = refs
    else:
        x_ref, g_ref, wg_ref, wu_ref, wo_ref, o_ref, xn_ref, acc_ref = refs
    k = pl.program_id(1)

    @pl.when(k == 0)
    def _():
        xn_ref[...] = _bf(_rms(x_ref[...], g_ref[...]))
        acc_ref[...] = jnp.zeros_like(acc_ref)

    xn = xn_ref[...]
    gate = _dot(xn, wg_ref[...])
    up = _dot(xn, wu_ref[...])
    act = _bf(gate * _sigmoid(gate) * up)
    acc_ref[...] += _dot(act, wo_ref[...])

    @pl.when(k == pl.num_programs(1) - 1)
    def _():
        out = x_ref[...] + acc_ref[...]
        if final:
            out = _rms(out, fg_ref[...])
        o_ref[...] = out


def ffn(x, gain, w_in, w_out, final_gain=None):
    m, d = x.shape
    hidden = w_out.shape[0]
    th = 256
    nk = hidden // th
    tm = min(m, 1024)
    final = final_gain is not None
    in_specs = [
        pl.BlockSpec((tm, d), lambda i, k: (i, 0)),
        pl.BlockSpec((1, d), lambda i, k: (0, 0)),
        pl.BlockSpec((d, th), lambda i, k: (0, k)),
        pl.BlockSpec((d, th), lambda i, k: (0, k + nk)),
        pl.BlockSpec((th, d), lambda i, k: (k, 0)),
    ]
    args = [x, gain.reshape(1, d), w_in, w_in, w_out]
    if final:
        in_specs.append(pl.BlockSpec((1, d), lambda i, k: (0, 0)))
        args.append(final_gain.reshape(1, d))
    return pl.pallas_call(
        functools.partial(_ffn_kernel, final=final),
        grid=(m // tm, nk),
        in_specs=in_specs,
        out_specs=pl.BlockSpec((tm, d), lambda i, k: (i, 0)),
        out_shape=jax.ShapeDtypeStruct((m, d), F32),
        scratch_shapes=[pltpu.VMEM((tm, d), BF16), pltpu.VMEM((tm, d), F32)],
        compiler_params=_params("parallel", "arbitrary"),
        name="swiglu",
    )(*args)


def _fox_flash_kernel(q_ref, k_ref, vt_ref, g_ref, o_ref, m_ref, acc_ref, s_buf, p_buf, a_buf, *, tq, tk):
    i = pl.program_id(2)
    dh = FOX_HEAD_DIM
    heads = range(2)
    n_full = (i * tq) // tk
    m_ref[...] = jnp.full_like(m_ref, NEG_BIG)
    acc_ref[...] = jnp.zeros_like(acc_ref)
    p_buf[...] = jnp.zeros_like(p_buf)
    a_buf[...] = jnp.ones_like(a_buf)
    key = lax.broadcasted_iota(jnp.int32, (tk, tq), 0)
    qry = lax.broadcasted_iota(jnp.int32, (tk, tq), 1)

    def scores(j, slot):
        off = pl.multiple_of(j * tk, tk)
        for hh in heads:
            s_buf[slot, hh] = _dot(k_ref[pl.ds(off, tk), hh * LANES:(hh + 1) * LANES], q_ref[hh])

    def softmax(j, slot, masked):
        ss = [s_buf[slot, hh] for hh in heads]
        if masked:
            ss = [jnp.where(key + j * tk <= qry + i * tq, s, NEG_BIG) for s in ss]
        m_prev = [m_ref[hh] for hh in heads]
        m_new = [jnp.maximum(m_prev[hh], jnp.max(ss[hh], axis=0, keepdims=True)) for hh in heads]
        for hh in heads:
            p_buf[slot, hh] = _bf(jnp.exp(ss[hh] - m_new[hh]))
            a_buf[slot, hh] = jnp.exp(m_prev[hh] - m_new[hh])
            m_ref[hh] = m_new[hh]

    def values(j, slot):
        off = pl.multiple_of(jnp.maximum(j, 0) * tk, tk)
        for hh in heads:
            acc_ref[hh] = a_buf[slot, hh] * acc_ref[hh] + _dot(vt_ref[hh, :, pl.ds(off, tk)], p_buf[slot, hh])

    scores(0, 0)

    def pair(u, carry):
        j = 2 * u
        scores(j + 1, 1)
        softmax(j, 0, False)
        values(j - 1, 1)
        scores(j + 2, 0)
        softmax(j + 1, 1, False)
        values(j, 0)
        return carry

    n_pairs = n_full // 2
    lax.fori_loop(0, n_pairs, pair, 0)
    j = 2 * n_pairs

    @pl.when(n_full - j == 1)
    def _():
        scores(j + 1, 1)
        softmax(j, 0, False)
        values(j - 1, 1)
        softmax(j + 1, 1, True)
        values(j, 0)
        values(j + 1, 1)

    @pl.when(n_full - j == 0)
    def _():
        softmax(j, 0, True)
        values(j - 1, 1)
        values(j, 0)

    halves = []
    for hh in range(2):
        a = acc_ref[hh].T
        halves.append(a[:, :dh] / a[:, dh:dh + 1])
    o_ref[...] = _bf(jnp.concatenate(halves, axis=-1) * _sigmoid(g_ref[...]))


def fox_flash(qt, ka, vt, gate_src, gate_col, *, batch, seq):
    m = ka.shape[0]
    n_pairs = vt.shape[1]
    tq = min(seq, FOX_Q_TILE)
    tk = min(seq, FOX_K_TILE)
    nq = seq // tq
    return pl.pallas_call(
        functools.partial(_fox_flash_kernel, tq=tq, tk=tk),
        grid=(batch, n_pairs, nq),
        in_specs=[
            pl.BlockSpec((None, None, 2, LANES, tq), lambda b, h, i: (b, h, 0, 0, i)),
            pl.BlockSpec((seq, 2 * LANES), lambda b, h, i: (b, h)),
            pl.BlockSpec((None, None, 2, LANES, seq), lambda b, h, i: (b, h, 0, 0, 0)),
            pl.BlockSpec((tq, LANES), lambda b, h, i: (b * nq + i, gate_col + h)),
        ],
        out_specs=pl.BlockSpec((tq, LANES), lambda b, h, i: (b * nq + i, h)),
        out_shape=jax.ShapeDtypeStruct((m, n_pairs * LANES), BF16),
        scratch_shapes=[pltpu.VMEM((2, 1, tq), F32), pltpu.VMEM((2, LANES, tq), F32),
                        pltpu.VMEM((2, 2, tk, tq), F32), pltpu.VMEM((2, 2, tk, tq), BF16),
                        pltpu.VMEM((2, 2, 1, tq), F32)],
        compiler_params=_params("parallel", "parallel", "arbitrary"),
        name="fox_flash",
    )(qt, ka, vt, gate_src)


def _fox_decode_kernel(pt_ref, q_ref, kn_ref, vn_ref, lfn_ref, g_ref, *refs, pps):
    lf_refs, k_refs, v_refs = refs[:pps], refs[pps:2 * pps], refs[2 * pps:3 * pps]
    o_ref, qb_ref, m_ref, l_ref, carry_ref, acc_ref = refs[3 * pps:]
    p = pl.program_id(1)
    n_heads, dh, page = k_refs[0].shape
    heads = range(n_heads)

    @pl.when(p == 0)
    def _():
        eye = (lax.broadcasted_iota(jnp.int32, (dh, dh), 0) == lax.broadcasted_iota(jnp.int32, (dh, dh), 1)).astype(F32)
        q_t = _dot_nt_hi(eye, q_ref[...])
        kn_t = _dot_nt_hi(eye, kn_ref[...])
        vn_t = _dot_nt_hi(eye, vn_ref[...])
        lane0 = lax.broadcasted_iota(jnp.int32, (dh, page), 1) == 0
        for h in heads:
            q_col = q_t[:, h:h + 1]
            qb_ref[h] = jnp.broadcast_to(q_col, (dh, page))
            s_new = jnp.sum(q_col * kn_t[:, h:h + 1], axis=0, keepdims=True)
            m_ref[h:h + 1, :] = jnp.broadcast_to(s_new, (1, page))
            acc_ref[h] = jnp.where(lane0, jnp.broadcast_to(vn_t[:, h:h + 1], (dh, page)), 0.0)
        l_ref[...] = jnp.ones_like(l_ref)
        carry_ref[...] = jnp.broadcast_to(lfn_ref[...], (n_heads, page))

    later = (lax.broadcasted_iota(jnp.int32, (page, page), 0) > lax.broadcasted_iota(jnp.int32, (page, page), 1)).astype(F32)
    carry = carry_ref[...]
    ss = []
    for i in range(pps):
        lf = lf_refs[i][...]
        bias = carry + _dot_hi(lf, later)
        carry = carry + jnp.sum(lf, axis=-1, keepdims=True)
        ss.append(jnp.concatenate([jnp.sum(qb_ref[h] * k_refs[i][h], axis=0, keepdims=True) for h in heads], axis=0)
                  + bias)
    carry_ref[...] = carry
    m_prev = m_ref[...]
    m_new = m_prev
    for s in ss:
        m_new = jnp.maximum(m_new, jnp.max(s, axis=-1, keepdims=True))
    alpha = jnp.exp(m_prev - m_new)
    prs = [jnp.exp(s - m_new) for s in ss]
    l_new = alpha * l_ref[...]
    for pr in prs:
        l_new = l_new + jnp.sum(pr, axis=-1, keepdims=True)
    l_ref[...] = l_new
    m_ref[...] = m_new
    for h in heads:
        acc = alpha[h:h + 1, :] * acc_ref[h]
        for i in range(pps):
            acc = acc + prs[i][h:h + 1, :] * v_refs[i][h]
        acc_ref[h] = acc

    @pl.when(p == pl.num_programs(1) - 1)
    def _():
        for h in heads:
            acc_ref[h] = acc_ref[h] / l_ref[h:h + 1, :]
        ones = jnp.ones((8, page), F32)
        o = _dot_nt_hi(ones, acc_ref[...].reshape(n_heads * dh, page))
        o_ref[...] = o[0:1, :] * g_ref[...]


def fox_decode(layer_idx, page_table, q, k_new, v_new, lf_new, gate, cache_logf_t, cache_k_t, cache_v_t):
    db, n_heads, dh = q.shape
    n_pages = page_table.shape[1]
    page = cache_k_t.shape[4]
    pps = DECODE_PAGES_PER_STEP if n_pages % DECODE_PAGES_PER_STEP == 0 else 1
    vec = pl.BlockSpec((None, n_heads, dh), lambda b, p, pt: (b, 0, 0))
    row = pl.BlockSpec((None, 1, n_heads * dh), lambda b, p, pt: (b, 0, 0))

    def past(i):
        return lambda b, p, pt: pt[b, n_pages - 1 - (p * pps + i)]

    lf_specs = [pl.BlockSpec((None, None, n_heads, page), lambda b, p, pt, f=past(i): (layer_idx, f(b, p, pt), 0, 0))
                for i in range(pps)]
    kv_specs = [pl.BlockSpec((None, None, n_heads, dh, page),
                             lambda b, p, pt, f=past(i): (layer_idx, f(b, p, pt), 0, 0, 0)) for i in range(pps)]
    return pl.pallas_call(
        functools.partial(_fox_decode_kernel, pps=pps),
        grid_spec=pltpu.PrefetchScalarGridSpec(
            num_scalar_prefetch=1,
            grid=(db, n_pages // pps),
            in_specs=[vec, vec, vec, pl.BlockSpec((None, n_heads, 1), lambda b, p, pt: (b, 0, 0)), row]
            + lf_specs + kv_specs + kv_specs,
            out_specs=row,
            scratch_shapes=[pltpu.VMEM((n_heads, dh, page), F32), pltpu.VMEM((n_heads, page), F32),
                            pltpu.VMEM((n_heads, page), F32), pltpu.VMEM((n_heads, page), F32),
                            pltpu.VMEM((n_heads, dh, page), F32)],
        ),
        out_shape=jax.ShapeDtypeStruct((db, 1, n_heads * dh), F32),
        compiler_params=_params("parallel", "arbitrary"),
        name="fox_decode",
    )(page_table, q, k_new, v_new, lf_new, gate, *([cache_logf_t] * pps), *([cache_k_t] * pps), *([cache_v_t] * pps))


def _tri_masks(c):
    row = lax.broadcasted_iota(jnp.int32, (c, c), 0)
    col = lax.broadcasted_iota(jnp.int32, (c, c), 1)
    return row >= col, row > col, row == col


def _cumsum_rows(tri_b, x):
    hi, mid, lo = _split3(x)
    return _dot(tri_b, hi) + (_dot(tri_b, mid) + _dot(tri_b, lo))


def _neumann_inverses(mats, eye, c):
    steps = max(c.bit_length() - 2, 0)
    ps = [eye + a for a in mats]
    if steps == 0:
        return ps
    splits = [_split2(a) for a in mats]
    aks = [_dot_x3(s, s) for s in splits]
    for step in range(steps):
        last = step == steps - 1
        nxt = []
        for p, ak in zip(ps, aks):
            ak2 = _split2(ak)
            lhs = p if last else jnp.concatenate([p, ak], axis=0)
            nxt.append(_dot_x3(_split2(lhs), ak2))
        ps = [p + n[:c] for p, n in zip(ps, nxt)]
        aks = [None if last else n[c:] for n in nxt]
    return ps


def _gdn_chunk_kernel(q_ref, k_ref, v_ref, z_ref, bcol_ref, gcol_ref, grow_ref, gain_ref, o_ref, s_out_ref, s_ref, *, c, hb):
    ch = pl.program_id(2)
    dk = GDN_HEAD_DIM

    @pl.when(ch == 0)
    def _():
        s_ref[...] = jnp.zeros_like(s_ref)

    incl, strict, diag = _tri_masks(c)
    eye = jnp.where(diag, 1.0, 0.0).astype(F32)
    tri_b = jnp.where(incl, 1.0, 0.0).astype(BF16)
    gc_col = _cumsum_rows(tri_b, gcol_ref[...])
    g3 = _split3(grow_ref[...])
    gc_row = _dot_nt(g3[0], tri_b) + (_dot_nt(g3[1], tri_b) + _dot_nt(g3[2], tri_b))
    heads = range(hb)
    vs = [slice(hh * dk, (hh + 1) * dk) for hh in heads]
    q = [q_ref[:, (hh // 2) * dk:(hh // 2 + 1) * dk] for hh in heads]
    k = [k_ref[:, (hh // 2) * dk:(hh // 2 + 1) * dk] for hh in heads]
    beta = [bcol_ref[:, hh:hh + 1] for hh in heads]
    gcc = [gc_col[:, hh:hh + 1] for hh in heads]
    g_last = [g[c - 1:c, :] for g in gcc]
    decay = [jnp.where(incl, jnp.exp(jnp.where(incl, gcc[hh] - gc_row[hh:hh + 1, :], 0.0)), 0.0) for hh in heads]
    kb = [k[hh] * beta[hh] for hh in heads]
    kbf = [_bf(x) for x in k]
    kk = [_dot_nt(_bf(kb[hh]), kbf[hh]) for hh in heads]
    qk = [_dot_nt(_bf(q[hh]), kbf[hh]) for hh in heads]
    neg_lower = [jnp.where(strict, -(kk[hh] * decay[hh]), 0.0) for hh in heads]
    attn = [_bf(jnp.where(incl, qk[hh] * decay[hh], 0.0)) for hh in heads]
    e_gc = [jnp.exp(g) for g in gcc]
    rhs = [_split2(jnp.concatenate([v_ref[:, vs[hh]] * beta[hh], kb[hh] * e_gc[hh]], axis=-1)) for hh in heads]
    q_dec = [_bf(q[hh] * e_gc[hh]) for hh in heads]
    k_dec = [_bf(k[hh] * jnp.exp(g_last[hh] - gcc[hh])) for hh in heads]
    t_inv = _neumann_inverses(neg_lower, eye, c)
    sol = [_dot_x3(_split2(t_inv[hh]), rhs[hh]) for hh in heads]
    s = [s_ref[hh] for hh in heads]
    sb = [_bf(x) for x in s]
    vnb = [_bf(sol[hh][:, :dk] - _dot(_bf(sol[hh][:, dk:]), sb[hh])) for hh in heads]
    o = [_dot(q_dec[hh], sb[hh]) + _dot(attn[hh], vnb[hh]) for hh in heads]
    s_new = [s[hh] * jnp.exp(g_last[hh]) + _dot_tn(k_dec[hh], vnb[hh]) for hh in heads]
    for hh in heads:
        s_ref[hh] = s_new[hh]
        z = z_ref[:, vs[hh]]
        o_ref[:, vs[hh]] = _bf(_rms(o[hh], gain_ref[...]) * (z * _sigmoid(z)))

    @pl.when(ch == pl.num_programs(2) - 1)
    def _():
        s_out_ref[...] = s_ref[...]


def gdn_chunk_scan(q, k, v, z, beta, g, gain, *, batch, seq):
    m = q.shape[0]
    n_heads = v.shape[1] // GDN_HEAD_DIM
    c = min(CHUNK, seq)
    hb = min(16, n_heads)
    n_groups = n_heads // hb
    n = seq // c
    bcol = beta.reshape(m, n_groups, hb).transpose(1, 0, 2)
    gcol = g.reshape(m, n_groups, hb).transpose(1, 0, 2)
    grow = g.reshape(batch * n, c, n_groups, hb).transpose(0, 2, 3, 1)
    qk_w = (hb // 2) * GDN_HEAD_DIM
    v_w = hb * GDN_HEAD_DIM
    row_blk = lambda b, h, t: (b * n + t, h)
    col_spec = pl.BlockSpec((None, c, hb), lambda b, h, t: (h, b * n + t, 0))
    return pl.pallas_call(
        functools.partial(_gdn_chunk_kernel, c=c, hb=hb),
        grid=(batch, n_groups, n),
        in_specs=[
            pl.BlockSpec((c, qk_w), row_blk),
            pl.BlockSpec((c, qk_w), row_blk),
            pl.BlockSpec((c, v_w), row_blk),
            pl.BlockSpec((c, v_w), row_blk),
            col_spec, col_spec,
            pl.BlockSpec((None, None, hb, c), lambda b, h, t: (b * n + t, h, 0, 0)),
            pl.BlockSpec((1, GDN_HEAD_DIM), lambda b, h, t: (0, 0)),
        ],
        out_specs=[
            pl.BlockSpec((c, v_w), row_blk),
            pl.BlockSpec((None, hb, GDN_HEAD_DIM, GDN_HEAD_DIM), lambda b, h, t: (b, h, 0, 0)),
        ],
        out_shape=[jax.ShapeDtypeStruct((m, n_heads * GDN_HEAD_DIM), BF16),
                   jax.ShapeDtypeStruct((batch, n_heads, GDN_HEAD_DIM, GDN_HEAD_DIM), F32)],
        scratch_shapes=[pltpu.VMEM((hb, GDN_HEAD_DIM, GDN_HEAD_DIM), F32)],
        compiler_params=_params("parallel", "parallel", "arbitrary"),
        name="gdn_chunk_scan",
    )(q, k, v, z, bcol, gcol, grow, gain.reshape(1, GDN_HEAD_DIM))


def _gdn_step_kernel(q_ref, k_ref, v_ref, z_ref, beta_ref, g_ref, gain_ref, s0_ref, o_ref, s_ref):
    dk = GDN_HEAD_DIM
    n_heads = s0_ref.shape[0]
    n_qk = q_ref.shape[0]
    rep = n_heads // n_qk
    q_rows = q_ref[...]
    k_rows = k_ref[...]
    eye = (lax.broadcasted_iota(jnp.int32, (dk, dk), 0) == lax.broadcasted_iota(jnp.int32, (dk, dk), 1)).astype(F32)
    cols = _dot_nt_hi(eye, jnp.concatenate([q_rows, k_rows], axis=0))
    qk_dot = jnp.sum(q_rows * k_rows, axis=-1, keepdims=True)
    for j in range(n_qk):
        q_c = jnp.broadcast_to(cols[:, j:j + 1], (dk, dk))
        k_c = jnp.broadcast_to(cols[:, n_qk + j:n_qk + j + 1], (dk, dk))
        for h in range(j * rep, (j + 1) * rep):
            vs = slice(h * dk, (h + 1) * dk)
            beta = beta_ref[:, h:h + 1]
            e_g = jnp.exp(g_ref[:, h:h + 1])
            s0 = s0_ref[h]
            w_s = jnp.sum(s0 * k_c, axis=0, keepdims=True) * (beta * e_g)
            q_s = jnp.sum(s0 * q_c, axis=0, keepdims=True) * e_g
            v_new = v_ref[:, vs] * beta - w_s
            o = q_s + qk_dot[j:j + 1, :] * v_new
            s_ref[h] = s0 * e_g + k_c * v_new
            z = z_ref[:, vs]
            o_ref[:, vs] = _rms(o, gain_ref[...]) * (z * _sigmoid(z))


def gdn_step(q, k, v, z, beta, g, gain, s0):
    db, n_qk, dk = q.shape
    n_heads = s0.shape[1]
    vec = lambda w: pl.BlockSpec((None, 1, w), lambda b: (b, 0, 0))
    rows = pl.BlockSpec((None, n_qk, dk), lambda b: (b, 0, 0))
    st = pl.BlockSpec((None, n_heads, dk, dk), lambda b: (b, 0, 0, 0))
    return pl.pallas_call(
        _gdn_step_kernel,
        grid=(db,),
        in_specs=[rows, rows, vec(v.shape[2]), vec(v.shape[2]), vec(n_heads), vec(n_heads),
                  pl.BlockSpec((1, dk), lambda b: (0, 0)), st],
        out_specs=[vec(v.shape[2]), st],
        out_shape=[jax.ShapeDtypeStruct(v.shape, F32), jax.ShapeDtypeStruct(s0.shape, F32)],
        compiler_params=_params("parallel"),
        name="gdn_step",
    )(q, k, v, z, beta, g, gain.reshape(1, dk), s0)


def _rwkv_post(y, r, k, v, gate, rk, lnw, lnb):
    mean = jnp.mean(y, axis=-1, keepdims=True)
    var = jnp.mean(jnp.square(y - mean), axis=-1, keepdims=True)
    y = (y - mean) * lax.rsqrt(var + RWKV_GN_EPS) * lnw + lnb
    y = y + jnp.sum(r * k * rk, axis=-1, keepdims=True) * v
    return y * gate


def _rwkv_keys(k, a, kk_w, ka_w):
    kk = k * kk_w
    kk = kk * lax.rsqrt(jnp.sum(kk * kk, axis=-1, keepdims=True) + 1e-6)
    k = k * (1.0 + (a - 1.0) * ka_w)
    return k, -kk, kk * a


def _rwkv_chunk_kernel(r_ref, lw_ref, k_ref, v_ref, a_ref, gate_ref, kk_ref, ka_ref, rk_ref, lnw_ref, lnb_ref,
                       o_ref, s_out_ref, s_ref, *, c, hb):
    ch = pl.program_id(2)
    n = RWKV_HEAD_DIM

    @pl.when(ch == 0)
    def _():
        s_ref[...] = jnp.zeros_like(s_ref)

    incl, strict, diag = _tri_masks(c)
    eye = jnp.where(diag, 1.0, 0.0).astype(F32)
    tri_b = jnp.where(incl, 1.0, 0.0).astype(BF16)
    gi_all = _cumsum_rows(tri_b, lw_ref[...])
    heads = range(hb)
    sl = [slice(hh * n, (hh + 1) * n) for hh in heads]
    r = [r_ref[:, s_] for s_ in sl]
    v = [v_ref[:, s_] for s_ in sl]
    vb = [_bf(x) for x in v]
    keys = [_rwkv_keys(k_ref[:, s_], a_ref[:, s_], kk_ref[:, s_], ka_ref[:, s_]) for s_ in sl]
    k = [t[0] for t in keys]
    a_vec = [t[1] for t in keys]
    b_vec = [t[2] for t in keys]
    gi = [gi_all[:, s_] for s_ in sl]
    g_last = [g[c - 1:c, :] for g in gi]
    e_neg = [jnp.exp(-g) for g in gi]
    e_rest = [jnp.exp(g_last[hh] - gi[hh]) for hh in heads]
    a_t = [a_vec[hh] * jnp.exp(gi[hh] - lw_ref[:, sl[hh]]) for hh in heads]
    r_tb = [_bf(r[hh] * jnp.exp(gi[hh])) for hh in heads]
    lhs = [jnp.concatenate([_bf(a_t[hh]), r_tb[hh]], axis=0) for hh in heads]
    rhs = [_bf(jnp.concatenate([b_vec[hh] * e_neg[hh], k[hh] * e_neg[hh]], axis=0)) for hh in heads]
    tail = [_bf(jnp.concatenate([b_vec[hh] * e_rest[hh], k[hh] * e_rest[hh]], axis=0)) for hh in heads]
    m4 = [_dot_nt(lhs[hh], rhs[hh]) for hh in heads]
    a_ab = [jnp.where(strict, m[:c, :c], 0.0) for m in m4]
    a_ak = [_bf(jnp.where(strict, m[:c, c:], 0.0)) for m in m4]
    a_rb = [_bf(jnp.where(incl, m[c:, :c], 0.0)) for m in m4]
    a_rk = [_bf(jnp.where(incl, m[c:, c:], 0.0)) for m in m4]
    akv = [_dot(a_ak[hh], vb[hh]) for hh in heads]
    t_inv = _neumann_inverses(a_ab, eye, c)
    sol = [_dot_x3(_split2(t_inv[hh]), _split2(jnp.concatenate([a_t[hh], akv[hh]], axis=-1))) for hh in heads]
    s = [s_ref[hh] for hh in heads]
    sb = [_bf(x) for x in s]
    ub = [_bf(_dot_nt(_bf(sol[hh][:, :n]), sb[hh]) + sol[hh][:, n:]) for hh in heads]
    y = [_dot_nt(r_tb[hh], sb[hh]) + _dot(a_rb[hh], ub[hh]) + _dot(a_rk[hh], vb[hh]) for hh in heads]
    s_new = [s[hh] * jnp.exp(g_last[hh]) + _dot_tn(jnp.concatenate([ub[hh], vb[hh]], axis=0), tail[hh])
             for hh in heads]
    for hh in heads:
        s_ref[hh] = s_new[hh]
        o_ref[:, sl[hh]] = _bf(_rwkv_post(y[hh], r[hh], k[hh], v[hh], gate_ref[:, sl[hh]], rk_ref[:, sl[hh]],
                                          lnw_ref[:, sl[hh]], lnb_ref[:, sl[hh]]))

    @pl.when(ch == pl.num_programs(2) - 1)
    def _():
        s_out_ref[...] = s_ref[...]


def rwkv_chunk_scan(r, lw, k, v, a, gate, kk_w, ka_w, rk, lnw, lnb, *, batch, seq):
    m, d = r.shape
    n_heads = d // RWKV_HEAD_DIM
    hb = min(16, n_heads)
    w = hb * RWKV_HEAD_DIM
    c = min(CHUNK, seq)
    n = seq // c
    tok = pl.BlockSpec((c, w), lambda b, h, t: (b * n + t, h))
    par = pl.BlockSpec((1, w), lambda b, h, t: (0, h))
    return pl.pallas_call(
        functools.partial(_rwkv_chunk_kernel, c=c, hb=hb),
        grid=(batch, n_heads // hb, n),
        in_specs=[tok] * 6 + [par] * 5,
        out_specs=[tok, pl.BlockSpec((None, hb, RWKV_HEAD_DIM, RWKV_HEAD_DIM), lambda b, h, t: (b, h, 0, 0))],
        out_shape=[jax.ShapeDtypeStruct((m, d), BF16),
                   jax.ShapeDtypeStruct((batch, n_heads, RWKV_HEAD_DIM, RWKV_HEAD_DIM), F32)],
        scratch_shapes=[pltpu.VMEM((hb, RWKV_HEAD_DIM, RWKV_HEAD_DIM), F32)],
        compiler_params=_params("parallel", "parallel", "arbitrary"),
        name="rwkv_chunk_scan",
    )(r, lw, k, v, a, gate, kk_w, ka_w, rk, lnw, lnb)


def _rwkv_step_kernel(r_ref, lw_ref, k_ref, v_ref, a_ref, gate_ref, kk_ref, ka_ref, rk_ref, lnw_ref, lnb_ref,
                      s0_ref, o_ref, s_ref, y_ref):
    n = s0_ref.shape[0]
    r = r_ref[...]
    v = v_ref[...]
    k0 = k_ref[...]
    a = a_ref[...]
    kk = k0 * kk_ref[...]
    kk = kk * lax.rsqrt(jnp.sum(kk * kk, axis=0, keepdims=True) + 1e-6)
    k = k0 * (1.0 + (a - 1.0) * ka_ref[...])
    a_vec = -kk
    b_vec = kk * a
    w = jnp.exp(lw_ref[...])

    def body(i, carry):
        s0 = s0_ref[i]
        sa = jnp.sum(s0 * a_vec, axis=0, keepdims=True)
        s = s0 * w + sa * b_vec + v_ref[pl.ds(i, 1), :] * k
        s_ref[i] = s
        y_ref[pl.ds(i, 1), :] = jnp.sum(s * r, axis=0, keepdims=True)
        return carry

    lax.fori_loop(0, n, body, 0, unroll=8)
    y = y_ref[...]
    mean = jnp.mean(y, axis=0, keepdims=True)
    var = jnp.mean(jnp.square(y - mean), axis=0, keepdims=True)
    y = (y - mean) * lax.rsqrt(var + RWKV_GN_EPS) * lnw_ref[...] + lnb_ref[...]
    y = y + jnp.sum(r * k * rk_ref[...], axis=0, keepdims=True) * v
    o_ref[...] = y * gate_ref[...]


def rwkv_step(r, lw, k, v, a, gate, kk_w, ka_w, rk, lnw, lnb, s0_t):
    d, db = r.shape
    n = RWKV_HEAD_DIM
    n_heads = d // n
    tok = pl.BlockSpec((n, db), lambda h: (h, 0))
    par = pl.BlockSpec((n, 1), lambda h: (h, 0))
    st = pl.BlockSpec((None, n, n, db), lambda h: (h, 0, 0, 0))
    return pl.pallas_call(
        _rwkv_step_kernel,
        grid=(n_heads,),
        in_specs=[tok] * 6 + [par] * 5 + [st],
        out_specs=[tok, st],
        out_shape=[jax.ShapeDtypeStruct((d, db), F32), jax.ShapeDtypeStruct(s0_t.shape, F32)],
        scratch_shapes=[pltpu.VMEM((n, db), F32)],
        compiler_params=_params("parallel"),
        name="rwkv_step",
    )(r, lw, k, v, a, gate, kk_w, ka_w, rk, lnw, lnb, s0_t)


def _head_rms(x, gain, n_heads):
    m = x.shape[0]
    xh = x.reshape(m, n_heads, -1)
    return (xh * lax.rsqrt(jnp.mean(xh * xh, axis=-1, keepdims=True) + NORM_EPS) * gain).reshape(m, -1)


def _fox_project(x, gain, w_in, b_f, q_gain, k_gain):
    width = (w_in.shape[1] // LANES) * LANES
    n_heads = w_in.shape[1] - width
    w = _bf(jnp.concatenate([w_in[:, :width], _pad_cols(w_in[:, width:], LANES)], axis=1))
    h = mm(x, w, gain=gain)
    d = width // 4
    q = _head_rms(h[:, :d], q_gain, n_heads) * (FOX_HEAD_DIM ** -0.5)
    k = _head_rms(h[:, d:2 * d], k_gain, n_heads)
    v = h[:, 2 * d:3 * d]
    gate = _sigmoid(h[:, 3 * d:4 * d])
    log_f = jax.nn.log_sigmoid(h[:, width:width + n_heads] + b_f)
    return q, k, v, gate, log_f


def _fox_prompt(x, gain, w_in, b_f, q_gain, k_gain, batch, seq):
    m = batch * seq
    dh = FOX_HEAD_DIM
    width = (w_in.shape[1] // LANES) * LANES
    n_heads = w_in.shape[1] - width
    d = width // 4
    qv_t = mm_t(x, _bf(jnp.concatenate([w_in[:, :d], w_in[:, 2 * d:3 * d]], axis=1).T), gain, batch=batch, seq=seq)
    rest = jnp.concatenate([w_in[:, d:2 * d], w_in[:, 3 * d:4 * d], _pad_cols(w_in[:, width:], 2 * LANES)], axis=1)
    h = mm(x, _bf(rest), gain=gain)
    k = _head_rms(h[:, :d], k_gain, n_heads)
    lf = jax.nn.log_sigmoid(h[:, 2 * d:2 * d + n_heads] + b_f)
    cum = jnp.cumsum(lf.reshape(batch, seq, n_heads), axis=1).reshape(m, n_heads)
    q_t = qv_t[:, :d].reshape(batch, n_heads, dh, seq)
    q_t = q_t * lax.rsqrt(jnp.mean(q_t * q_t, axis=2, keepdims=True) + NORM_EPS) * (q_gain * dh ** -0.5)[:, None]
    v_t = qv_t[:, d:].reshape(batch, n_heads, dh, seq)
    rows = lambda n, val: jnp.full((batch, n_heads, n, seq), val, BF16)
    qt = jnp.concatenate([_bf(q_t), rows(3, 1.0), rows(LANES - dh - 3, 0.0)], axis=2)
    vt = jnp.concatenate([_bf(v_t), rows(1, 1.0), rows(LANES - dh - 1, 0.0)], axis=2)
    neg_c3 = jnp.stack(_split3(-cum), axis=-1)
    ka = jnp.concatenate([_bf(k).reshape(m, n_heads, dh), neg_c3, jnp.zeros((m, n_heads, LANES - dh - 3), BF16)],
                         axis=-1).reshape(m, n_heads * LANES)
    pair = lambda a: a.reshape(batch, n_heads // 2, 2, LANES, seq)
    return pair(qt), ka, pair(vt), h, d // LANES, k, v_t.transpose(0, 3, 1, 2), lf


def fox_layer(hp, hs, gain, w_in, b_f, q_gain, k_gain, w_out, layer_idx, cache_k_t, cache_v_t, cache_logf_t,
              page_table, batch, seq):
    n_heads = b_f.shape[0]
    dh = FOX_HEAD_DIM
    w_out_b = _bf(w_out)
    qt, ka, vt, gate_src, gate_col, k, v, lf = _fox_prompt(hp, gain, w_in, b_f, q_gain, k_gain, batch, seq)
    og = fox_flash(qt, ka, vt, gate_src, gate_col, batch=batch, seq=seq)
    hp = mm(og, w_out_b, res=hp)
    outs_p = (k.reshape(batch, seq, n_heads, dh), v, lf.reshape(batch, seq, n_heads))
    db = hs.shape[0]
    q, k, v, gate, lf = _fox_project(hs, gain, w_in, b_f, q_gain, k_gain)
    og = fox_decode(layer_idx, page_table, q.reshape(db, n_heads, dh), k.reshape(db, n_heads, dh),
                    v.reshape(db, n_heads, dh), lf.reshape(db, n_heads, 1), gate.reshape(db, 1, n_heads * dh),
                    cache_logf_t, cache_k_t, cache_v_t)
    hs = mm(og.reshape(db, n_heads * dh), w_out_b, res=hs)
    outs_s = (k.reshape(db, 1, n_heads, dh), v.reshape(db, 1, n_heads, dh), lf.reshape(db, 1, n_heads))
    return hp, hs, outs_p, outs_s


def _gdn_project(x, gain, w_in, n_heads):
    total = w_in.shape[1]
    main = total - 2 * n_heads
    w = _bf(jnp.concatenate([w_in[:, :main], _pad_cols(w_in[:, main:], LANES)], axis=1))
    h = mm(x, w, gain=gain)
    conv_ch = main - n_heads * GDN_HEAD_DIM
    return h[:, :conv_ch], h[:, conv_ch:main], h[:, main:main + n_heads], h[:, main + n_heads:main + 2 * n_heads]


def _gdn_post_conv(conv, a, beta_logit, a_log, dt_bias, n_heads):
    m = conv.shape[0]
    conv = conv * _sigmoid(conv)
    qk_w = (n_heads // 2) * GDN_HEAD_DIM

    def l2(x):
        xh = x.reshape(m, n_heads // 2, GDN_HEAD_DIM)
        return (xh * lax.rsqrt(jnp.sum(xh * xh, axis=-1, keepdims=True) + 1e-6)).reshape(m, qk_w)

    q = l2(conv[:, :qk_w]) * (GDN_HEAD_DIM ** -0.5)
    k = l2(conv[:, qk_w:2 * qk_w])
    v = conv[:, 2 * qk_w:]
    beta = _sigmoid(beta_logit)
    g = -jnp.exp(a_log) * jax.nn.softplus(a + dt_bias)
    return q, k, v, beta, g


def gdn_layer(hp, hs, gain, w_in, conv_w, a_log, dt_bias, out_gain, w_out, conv_state, s_state, batch, seq):
    n_heads = a_log.shape[0]
    w_out_b = _bf(w_out)
    qkv, z, a, bl = _gdn_project(hp, gain, w_in, n_heads)
    ch = qkv.shape[1]
    xc = jnp.concatenate([jnp.zeros((batch, GDN_CONV_W - 1, ch), F32), qkv.reshape(batch, seq, ch)], axis=1)
    conv = xc[:, 0:seq] * conv_w[0]
    for i in range(1, GDN_CONV_W):
        conv = conv + xc[:, i:i + seq] * conv_w[i]
    q, k, v, beta, g = _gdn_post_conv(conv.reshape(batch * seq, ch), a, bl, a_log, dt_bias, n_heads)
    og, s_p = gdn_chunk_scan(q, k, v, z, beta, g, out_gain, batch=batch, seq=seq)
    hp = mm(og, w_out_b, res=hp)
    conv_p = xc[:, seq:]
    db = hs.shape[0]
    qkv, z, a, bl = _gdn_project(hs, gain, w_in, n_heads)
    xc = jnp.concatenate([conv_state, qkv[:, None, :]], axis=1)
    conv = xc[:, 0] * conv_w[0]
    for i in range(1, GDN_CONV_W):
        conv = conv + xc[:, i] * conv_w[i]
    q, k, v, beta, g = _gdn_post_conv(conv, a, bl, a_log, dt_bias, n_heads)
    og, s_s = gdn_step(q.reshape(db, -1, GDN_HEAD_DIM), k.reshape(db, -1, GDN_HEAD_DIM), v[:, None], z[:, None],
                       beta[:, None], g[:, None], out_gain, s_state)
    hs = mm(og.reshape(db, -1), w_out_b, res=hs)
    conv_s = xc[:, 1:]
    return hp, hs, (s_p, conv_p), (s_s, conv_s)


def _rwkv_project(u, u_prev, mu, w0, w1, w2, a0, a1, a2, g1, g2, w_r, w_k, w_v):
    du = u_prev - u
    xr, xw, xk, xv, xa, xg = (u + du * mu[i] for i in range(6))
    lora = lambda w: -(-w.shape[1] // LANES) * LANES
    r = mm(xr, _bf(w_r))
    k = mm(xk, _bf(w_k))
    v = mm(xv, _bf(w_v))
    nw, na, ng = lora(w1), lora(a1), lora(g1)
    w_l = mm(jnp.tanh(mm(xw, _bf(_pad_cols(w1, nw)))), _bf(_pad_rows(w2, nw)))
    a_l = mm(mm(xa, _bf(_pad_cols(a1, na))), _bf(_pad_rows(a2, na)))
    gate = mm(_sigmoid(mm(xg, _bf(_pad_cols(g1, ng)))), _bf(_pad_rows(g2, ng)))
    w_raw = -jax.nn.softplus(-(w0 + w_l)) - 0.5
    log_decay = -jnp.exp(w_raw)
    a = _sigmoid(a0 + a_l)
    return r, log_decay, k, v, a, gate


def rwkv_layer(hp, hs, gain, mu, w0, w1, w2, a0, a1, a2, g1, g2, k_k, k_a, r_k, w_r, w_k, w_v, w_o, ln_w, ln_b,
               shift_state, wkv_state, batch, seq):
    d = hp.shape[1]
    w_o_b = _bf(w_o)
    proj_w = (mu, w0, w1, w2, a0, a1, a2, g1, g2, w_r, w_k, w_v)
    chan = (k_k, k_a, r_k.reshape(d), ln_w, ln_b)
    u = _rms(hp, gain).reshape(batch, seq, d)
    u_prev = jnp.concatenate([jnp.zeros((batch, 1, d), F32), u[:, :-1]], axis=1)
    toks = _rwkv_project(u.reshape(batch * seq, d), u_prev.reshape(batch * seq, d), *proj_w)
    og, s_p = rwkv_chunk_scan(*toks, *(p.reshape(1, d) for p in chan), batch=batch, seq=seq)
    hp = mm(og, w_o_b, res=hp)
    shift_p = u[:, -1]
    us = _rms(hs, gain)
    toks = _rwkv_project(us, shift_state, *proj_w)
    og_t, s_t = rwkv_step(*(t.T for t in toks), *(p.reshape(d, 1) for p in chan), wkv_state.transpose(1, 2, 3, 0))
    hs = mm(og_t.T, w_o_b, res=hs)
    return hp, hs, (shift_p, s_p), (us, s_t.transpose(3, 0, 1, 2))


def kernel(x_prompt, x_sample, cache_k, cache_v, cache_logf, page_table, state_gdn_s, state_gdn_conv, state_rwkv_shift, state_rwkv_wkv, norm_mix, norm_ffn, norm_final, fox_w_in, fox_b_f, fox_q_norm, fox_k_norm, fox_w_out, gdn_w_in, gdn_conv_w, gdn_a_log, gdn_dt_bias, gdn_out_norm, gdn_w_out, rwkv_mu, rwkv_w0, rwkv_w1, rwkv_w2, rwkv_a0, rwkv_a1, rwkv_a2, rwkv_g1, rwkv_g2, rwkv_k_k, rwkv_k_a, rwkv_r_k, rwkv_w_r, rwkv_w_k, rwkv_w_v, rwkv_w_o, rwkv_ln_w, rwkv_ln_b, ffn_w_in, ffn_w_out):
    batch, seq, d = x_prompt.shape
    db = x_sample.shape[0]
    depth = norm_mix.shape[0]
    hp = x_prompt.reshape(batch * seq, d)
    hs = x_sample.reshape(db, d)
    cache_k_t = cache_k.transpose(0, 1, 3, 4, 2)
    cache_v_t = cache_v.transpose(0, 1, 3, 4, 2)
    cache_logf_t = cache_logf.transpose(0, 1, 3, 2)
    fox_p, fox_s, gdn_p, gdn_s, rwkv_p, rwkv_s = [], [], [], [], [], []
    for layer in range(depth):
        kind, j = layer % 3, layer // 3
        if kind == 0:
            hp, hs, op, os_ = fox_layer(hp, hs, norm_mix[layer], fox_w_in[j], fox_b_f[j], fox_q_norm[j], fox_k_norm[j],
                                        fox_w_out[j], j, cache_k_t, cache_v_t, cache_logf_t, page_table, batch, seq)
            fox_p.append(op)
            fox_s.append(os_)
        elif kind == 1:
            hp, hs, op, os_ = gdn_layer(hp, hs, norm_mix[layer], gdn_w_in[j], gdn_conv_w[j], gdn_a_log[j],
                                        gdn_dt_bias[j], gdn_out_norm[j], gdn_w_out[j], state_gdn_conv[j],
                                        state_gdn_s[j], batch, seq)
            gdn_p.append(op)
            gdn_s.append(os_)
        else:
            hp, hs, op, os_ = rwkv_layer(hp, hs, norm_mix[layer], rwkv_mu[j], rwkv_w0[j], rwkv_w1[j], rwkv_w2[j],
                                         rwkv_a0[j], rwkv_a1[j], rwkv_a2[j], rwkv_g1[j], rwkv_g2[j], rwkv_k_k[j],
                                         rwkv_k_a[j], rwkv_r_k[j], rwkv_w_r[j], rwkv_w_k[j], rwkv_w_v[j],
                                         rwkv_w_o[j], rwkv_ln_w[j], rwkv_ln_b[j], state_rwkv_shift[j],
                                         state_rwkv_wkv[j], batch, seq)
            rwkv_p.append(op)
            rwkv_s.append(os_)
        final = norm_final if layer == depth - 1 else None
        w_in_b, w_out_b = _bf(ffn_w_in[layer]), _bf(ffn_w_out[layer])
        hp = ffn(hp, norm_ffn[layer], w_in_b, w_out_b, final)
        hs = ffn(hs, norm_ffn[layer], w_in_b, w_out_b, final)
    stack = lambda items, i: jnp.stack([it[i] for it in items])
    return (hp.reshape(batch, seq, d), hs.reshape(db, 1, d),
            stack(fox_p, 0), stack(fox_p, 1), stack(fox_p, 2),
            stack(fox_s, 0), stack(fox_s, 1), stack(fox_s, 2),
            stack(gdn_p, 0), stack(gdn_p, 1), stack(gdn_s, 0), stack(gdn_s, 1),
            stack(rwkv_p, 0), stack(rwkv_p, 1), stack(rwkv_s, 0), stack(rwkv_s, 1))
```

```python
import functools

import jax
import jax.numpy as jnp
from jax import lax
from jax.experimental import pallas as pl
from jax.experimental.pallas import tpu as pltpu

F32 = jnp.float32
BF16 = jnp.bfloat16
HIGHEST = lax.Precision.HIGHEST

NORM_EPS = 1e-6
RWKV_GN_EPS = 64e-5
LANES = 128
VMEM_LIMIT_BYTES = 48 * 1024 * 1024
NEG_BIG = -1e30

FOX_HEAD_DIM = 64
GDN_HEAD_DIM = 128
GDN_CONV_W = 4
RWKV_HEAD_DIM = 64
CHUNK = 64
FOX_Q_TILE = 512
FOX_K_TILE = 512
DECODE_PAGES_PER_STEP = 16


def _params(*semantics):
    return pltpu.CompilerParams(dimension_semantics=semantics, vmem_limit_bytes=VMEM_LIMIT_BYTES)


def _sigmoid(x):
    return 1.0 / (1.0 + jnp.exp(-x))


def _rms(x, gain):
    return x * lax.rsqrt(jnp.mean(x * x, axis=-1, keepdims=True) + NORM_EPS) * gain


def _dot(a, b):
    return jnp.dot(a, b, preferred_element_type=F32)


def _dot_nt(a, b):
    return lax.dot_general(a, b, (((1,), (1,)), ((), ())), preferred_element_type=F32)


def _dot_tn(a, b):
    return lax.dot_general(a, b, (((0,), (0,)), ((), ())), preferred_element_type=F32)


def _dot_hi(a, b):
    return jnp.dot(a, b, preferred_element_type=F32, precision=HIGHEST)


def _dot_nt_hi(a, b):
    return lax.dot_general(a, b, (((1,), (1,)), ((), ())), preferred_element_type=F32, precision=HIGHEST)


def _dot_tn_hi(a, b):
    return lax.dot_general(a, b, (((0,), (0,)), ((), ())), preferred_element_type=F32, precision=HIGHEST)


def _bf(x):
    return x.astype(BF16)


def _top_bits(x):
    bits = lax.bitcast_convert_type(x, jnp.int32) & jnp.int32(-65536)
    return lax.bitcast_convert_type(bits, F32)


def _split2(x):
    hi = _top_bits(x)
    return _bf(hi), _bf(x - hi)


def _split3(x):
    hi = _top_bits(x)
    r = x - hi
    mid = _top_bits(r)
    return _bf(hi), _bf(mid), _bf(r - mid)


def _dot_x3(a2, b2):
    (ah, al), (bh, bl) = a2, b2
    return _dot(ah, bh) + (_dot(ah, bl) + _dot(al, bh))


def _pad_cols(w, n):
    return jnp.pad(w, ((0, 0), (0, n - w.shape[1])))


def _pad_rows(w, n):
    return jnp.pad(w, ((0, n - w.shape[0]), (0, 0)))


def _col_tile(n, cap=1536):
    best = LANES
    for t in range(LANES, min(n, cap) + 1, LANES):
        if n % t == 0:
            best = t
    return best


def _mm_kernel(*refs, norm, residual):
    it = iter(refs)
    x_ref = next(it)
    g_ref = next(it) if norm else None
    w_ref = next(it)
    r_ref = next(it) if residual else None
    o_ref = next(it)
    xn_ref = next(it) if norm else None
    if norm:
        @pl.when(pl.program_id(1) == 0)
        def _():
            xn_ref[...] = _bf(_rms(x_ref[...], g_ref[...]))
        a = xn_ref[...]
    else:
        a = _bf(x_ref[...])
    acc = _dot(a, w_ref[...])
    if residual:
        acc = acc + r_ref[...]
    o_ref[...] = acc.astype(o_ref.dtype)


def mm(x, w, *, gain=None, res=None, out_dtype=F32):
    m, k = x.shape
    n = w.shape[1]
    tm = min(m, 1024)
    tn = _col_tile(n)
    norm, residual = gain is not None, res is not None
    in_specs = [pl.BlockSpec((tm, k), lambda i, j: (i, 0))]
    args = [x]
    if norm:
        in_specs.append(pl.BlockSpec((1, k), lambda i, j: (0, 0)))
        args.append(gain.reshape(1, k))
    in_specs.append(pl.BlockSpec((k, tn), lambda i, j: (0, j)))
    args.append(w)
    if residual:
        in_specs.append(pl.BlockSpec((tm, tn), lambda i, j: (i, j)))
        args.append(res)
    return pl.pallas_call(
        functools.partial(_mm_kernel, norm=norm, residual=residual),
        grid=(m // tm, n // tn),
        in_specs=in_specs,
        out_specs=pl.BlockSpec((tm, tn), lambda i, j: (i, j)),
        out_shape=jax.ShapeDtypeStruct((m, n), out_dtype),
        scratch_shapes=[pltpu.VMEM((tm, k), BF16)] if norm else [],
        compiler_params=_params("parallel", "arbitrary"),
        name="dense",
    )(*args)


def _mm_t_kernel(x_ref, g_ref, w_ref, o_ref, xn_ref):
    @pl.when(pl.program_id(1) == 0)
    def _():
        xn_ref[...] = _bf(_rms(x_ref[...], g_ref[...]))

    o_ref[...] = _dot_nt(w_ref[...], xn_ref[...])


def mm_t(x, w_t, gain, *, batch, seq):
    m, k = x.shape
    n = w_t.shape[0]
    tm = min(seq, 512)
    tn = min(n, 1024)
    nt = seq // tm
    return pl.pallas_call(
        _mm_t_kernel,
        grid=(m // tm, n // tn),
        in_specs=[pl.BlockSpec((tm, k), lambda i, j: (i, 0)),
                  pl.BlockSpec((1, k), lambda i, j: (0, 0)),
                  pl.BlockSpec((tn, k), lambda i, j: (j, 0))],
        out_specs=pl.BlockSpec((None, tn, tm), lambda i, j: (i // nt, j, i % nt)),
        out_shape=jax.ShapeDtypeStruct((batch, n, seq), F32),
        scratch_shapes=[pltpu.VMEM((tm, k), BF16)],
        compiler_params=_params("parallel", "arbitrary"),
        name="dense_t",
    )(x, gain.reshape(1, k), w_t)


def _ffn_kernel(*refs, final):
    if final:
        x_ref, g_ref, wg_ref, wu_ref, wo_ref, fg_ref, o_ref, xn_ref, acc_ref = refs
    else:
        x_ref, g_ref, wg_ref, wu_ref, wo_ref, o_ref, xn_ref, acc_ref = refs
    k = pl.program_id(1)

    @pl.when(k == 0)
    def _():
        xn_ref[...] = _bf(_rms(x_ref[...], g_ref[...]))
        acc_ref[...] = jnp.zeros_like(acc_ref)

    xn = xn_ref[...]
    gate = _dot(xn, wg_ref[...])
    up = _dot(xn, wu_ref[...])
    act = _bf(gate * _sigmoid(gate) * up)
    acc_ref[...] += _dot(act, wo_ref[...])

    @pl.when(k == pl.num_programs(1) - 1)
    def _():
        out = x_ref[...] + acc_ref[...]
        if final:
            out = _rms(out, fg_ref[...])
        o_ref[...] = out


def ffn(x, gain, w_in, w_out, final_gain=None):
    m, d = x.shape
    hidden = w_out.shape[0]
    th = 256
    nk = hidden // th
    tm = min(m, 1024)
    final = final_gain is not None
    in_specs = [
        pl.BlockSpec((tm, d), lambda i, k: (i, 0)),
        pl.BlockSpec((1, d), lambda i, k: (0, 0)),
        pl.BlockSpec((d, th), lambda i, k: (0, k)),
        pl.BlockSpec((d, th), lambda i, k: (0, k + nk)),
        pl.BlockSpec((th, d), lambda i, k: (k, 0)),
    ]
    args = [x, gain.reshape(1, d), w_in, w_in, w_out]
    if final:
        in_specs.append(pl.BlockSpec((1, d), lambda i, k: (0, 0)))
        args.append(final_gain.reshape(1, d))
    return pl.pallas_call(
        functools.partial(_ffn_kernel, final=final),
        grid=(m // tm, nk),
        in_specs=in_specs,
        out_specs=pl.BlockSpec((tm, d), lambda i, k: (i, 0)),
        out_shape=jax.ShapeDtypeStruct((m, d), F32),
        scratch_shapes=[pltpu.VMEM((tm, d), BF16), pltpu.VMEM((tm, d), F32)],
        compiler_params=_params("parallel", "arbitrary"),
        name="swiglu",
    )(*args)


def _fox_flash_kernel(q_ref, k_ref, vt_ref, g_ref, o_ref, m_ref, acc_ref, s_buf, p_buf, a_buf, *, tq, tk):
    i = pl.program_id(2)
    dh = FOX_HEAD_DIM
    heads = range(2)
    n_full = (i * tq) // tk
    m_ref[...] = jnp.full_like(m_ref, NEG_BIG)
    acc_ref[...] = jnp.zeros_like(acc_ref)
    p_buf[...] = jnp.zeros_like(p_buf)
    a_buf[...] = jnp.ones_like(a_buf)
    key = lax.broadcasted_iota(jnp.int32, (tk, tq), 0)
    qry = lax.broadcasted_iota(jnp.int32, (tk, tq), 1)

    def scores(j, slot):
        off = pl.multiple_of(j * tk, tk)
        for hh in heads:
            s_buf[slot, hh] = _dot(k_ref[pl.ds(off, tk), hh * LANES:(hh + 1) * LANES], q_ref[hh])

    def softmax(j, slot, masked):
        ss = [s_buf[slot, hh] for hh in heads]
        if masked:
            ss = [jnp.where(key + j * tk <= qry + i * tq, s, NEG_BIG) for s in ss]
        m_prev = [m_ref[hh] for hh in heads]
        m_new = [jnp.maximum(m_prev[hh], jnp.max(ss[hh], axis=0, keepdims=True)) for hh in heads]
        for hh in heads:
            p_buf[slot, hh] = _bf(jnp.exp(ss[hh] - m_new[hh]))
            a_buf[slot, hh] = jnp.exp(m_prev[hh] - m_new[hh])
            m_ref[hh] = m_new[hh]

    def values(j, slot):
        off = pl.multiple_of(jnp.maximum(j, 0) * tk, tk)
        for hh in heads:
            acc_ref[hh] = a_buf[slot, hh] * acc_ref[hh] + _dot(vt_ref[hh, :, pl.ds(off, tk)], p_buf[slot, hh])

    scores(0, 0)

    def pair(u, carry):
        j = 2 * u
        scores(j + 1, 1)
        softmax(j, 0, False)
        values(j - 1, 1)
        scores(j + 2, 0)
        softmax(j + 1, 1, False)
        values(j, 0)
        return carry

    n_pairs = n_full // 2
    lax.fori_loop(0, n_pairs, pair, 0)
    j = 2 * n_pairs

    @pl.when(n_full - j == 1)
    def _():
        scores(j + 1, 1)
        softmax(j, 0, False)
        values(j - 1, 1)
        softmax(j + 1, 1, True)
        values(j, 0)
        values(j + 1, 1)

    @pl.when(n_full - j == 0)
    def _():
        softmax(j, 0, True)
        values(j - 1, 1)
        values(j, 0)

    halves = []
    for hh in range(2):
        a = acc_ref[hh].T
        halves.append(a[:, :dh] / a[:, dh:dh + 1])
    o_ref[...] = _bf(jnp.concatenate(halves, axis=-1) * _sigmoid(g_ref[...]))


def fox_flash(qt, ka, vt, gate_src, gate_col, *, batch, seq):
    m = ka.shape[0]
    n_pairs = vt.shape[1]
    tq = min(seq, FOX_Q_TILE)
    tk = min(seq, FOX_K_TILE)
    nq = seq // tq
    return pl.pallas_call(
        functools.partial(_fox_flash_kernel, tq=tq, tk=tk),
        grid=(batch, n_pairs, nq),
        in_specs=[
            pl.BlockSpec((None, None, 2, LANES, tq), lambda b, h, i: (b, h, 0, 0, i)),
            pl.BlockSpec((seq, 2 * LANES), lambda b, h, i: (b, h)),
            pl.BlockSpec((None, None, 2, LANES, seq), lambda b, h, i: (b, h, 0, 0, 0)),
            pl.BlockSpec((tq, LANES), lambda b, h, i: (b * nq + i, gate_col + h)),
        ],
        out_specs=pl.BlockSpec((tq, LANES), lambda b, h, i: (b * nq + i, h)),
        out_shape=jax.ShapeDtypeStruct((m, n_pairs * LANES), BF16),
        scratch_shapes=[pltpu.VMEM((2, 1, tq), F32), pltpu.VMEM((2, LANES, tq), F32),
                        pltpu.VMEM((2, 2, tk, tq), F32), pltpu.VMEM((2, 2, tk, tq), BF16),
                        pltpu.VMEM((2, 2, 1, tq), F32)],
        compiler_params=_params("parallel", "parallel", "arbitrary"),
        name="fox_flash",
    )(qt, ka, vt, gate_src)


def _fox_decode_kernel(pt_ref, q_ref, kn_ref, vn_ref, lfn_ref, g_ref, *refs, pps):
    lf_refs, k_refs, v_refs = refs[:pps], refs[pps:2 * pps], refs[2 * pps:3 * pps]
    o_ref, qb_ref, m_ref, l_ref, carry_ref, acc_ref = refs[3 * pps:]
    p = pl.program_id(1)
    n_heads, dh, page = k_refs[0].shape
    heads = range(n_heads)

    @pl.when(p == 0)
    def _():
        eye = (lax.broadcasted_iota(jnp.int32, (dh, dh), 0) == lax.broadcasted_iota(jnp.int32, (dh, dh), 1)).astype(F32)
        q_t = _dot_nt_hi(eye, q_ref[...])
        kn_t = _dot_nt_hi(eye, kn_ref[...])
        vn_t = _dot_nt_hi(eye, vn_ref[...])
        lane0 = lax.broadcasted_iota(jnp.int32, (dh, page), 1) == 0
        for h in heads:
            q_col = q_t[:, h:h + 1]
            qb_ref[h] = jnp.broadcast_to(q_col, (dh, page))
            s_new = jnp.sum(q_col * kn_t[:, h:h + 1], axis=0, keepdims=True)
            m_ref[h:h + 1, :] = jnp.broadcast_to(s_new, (1, page))
            acc_ref[h] = jnp.where(lane0, jnp.broadcast_to(vn_t[:, h:h + 1], (dh, page)), 0.0)
        l_ref[...] = jnp.ones_like(l_ref)
        carry_ref[...] = jnp.broadcast_to(lfn_ref[...], (n_heads, page))

    later = (lax.broadcasted_iota(jnp.int32, (page, page), 0) > lax.broadcasted_iota(jnp.int32, (page, page), 1)).astype(F32)
    carry = carry_ref[...]
    ss = []
    for i in range(pps):
        lf = lf_refs[i][...]
        bias = carry + _dot_hi(lf, later)
        carry = carry + jnp.sum(lf, axis=-1, keepdims=True)
        ss.append(jnp.concatenate([jnp.sum(qb_ref[h] * k_refs[i][h], axis=0, keepdims=True) for h in heads], axis=0)
                  + bias)
    carry_ref[...] = carry
    m_prev = m_ref[...]
    m_new = m_prev
    for s in ss:
        m_new = jnp.maximum(m_new, jnp.max(s, axis=-1, keepdims=True))
    alpha = jnp.exp(m_prev - m_new)
    prs = [jnp.exp(s - m_new) for s in ss]
    l_new = alpha * l_ref[...]
    for pr in prs:
        l_new = l_new + jnp.sum(pr, axis=-1, keepdims=True)
    l_ref[...] = l_new
    m_ref[...] = m_new
    for h in heads:
        acc = alpha[h:h + 1, :] * acc_ref[h]
        for i in range(pps):
            acc = acc + prs[i][h:h + 1, :] * v_refs[i][h]
        acc_ref[h] = acc

    @pl.when(p == pl.num_programs(1) - 1)
    def _():
        for h in heads:
            acc_ref[h] = acc_ref[h] / l_ref[h:h + 1, :]
        ones = jnp.ones((8, page), F32)
        o = _dot_nt_hi(ones, acc_ref[...].reshape(n_heads * dh, page))
        o_ref[...] = o[0:1, :] * g_ref[...]


def fox_decode(layer_idx, page_table, q, k_new, v_new, lf_new, gate, cache_logf_t, cache_k_t, cache_v_t):
    db, n_heads, dh = q.shape
    n_pages = page_table.shape[1]
    page = cache_k_t.shape[4]
    pps = DECODE_PAGES_PER_STEP if n_pages % DECODE_PAGES_PER_STEP == 0 else 1
    vec = pl.BlockSpec((None, n_heads, dh), lambda b, p, pt: (b, 0, 0))
    row = pl.BlockSpec((None, 1, n_heads * dh), lambda b, p, pt: (b, 0, 0))

    def past(i):
        return lambda b, p, pt: pt[b, n_pages - 1 - (p * pps + i)]

    lf_specs = [pl.BlockSpec((None, None, n_heads, page), lambda b, p, pt, f=past(i): (layer_idx, f(b, p, pt), 0, 0))
                for i in range(pps)]
    kv_specs = [pl.BlockSpec((None, None, n_heads, dh, page),
                             lambda b, p, pt, f=past(i): (layer_idx, f(b, p, pt), 0, 0, 0)) for i in range(pps)]
    return pl.pallas_call(
        functools.partial(_fox_decode_kernel, pps=pps),
        grid_spec=pltpu.PrefetchScalarGridSpec(
            num_scalar_prefetch=1,
            grid=(db, n_pages // pps),
            in_specs=[vec, vec, vec, pl.BlockSpec((None, n_heads, 1), lambda b, p, pt: (b, 0, 0)), row]
            + lf_specs + kv_specs + kv_specs,
            out_specs=row,
            scratch_shapes=[pltpu.VMEM((n_heads, dh, page), F32), pltpu.VMEM((n_heads, page), F32),
                            pltpu.VMEM((n_heads, page), F32), pltpu.VMEM((n_heads, page), F32),
                            pltpu.VMEM((n_heads, dh, page), F32)],
        ),
        out_shape=jax.ShapeDtypeStruct((db, 1, n_heads * dh), F32),
        compiler_params=_params("parallel", "arbitrary"),
        name="fox_decode",
    )(page_table, q, k_new, v_new, lf_new, gate, *([cache_logf_t] * pps), *([cache_k_t] * pps), *([cache_v_t] * pps))


def _tri_masks(c):
    row = lax.broadcasted_iota(jnp.int32, (c, c), 0)
    col = lax.broadcasted_iota(jnp.int32, (c, c), 1)
    return row >= col, row > col, row == col


def _cumsum_rows(tri_b, x):
    hi, mid, lo = _split3(x)
    return _dot(tri_b, hi) + (_dot(tri_b, mid) + _dot(tri_b, lo))


def _neumann_inverses(mats, eye, c):
    steps = max(c.bit_length() - 2, 0)
    ps = [eye + a for a in mats]
    if steps == 0:
        return ps
    splits = [_split2(a) for a in mats]
    aks = [_dot_x3(s, s) for s in splits]
    for step in range(steps):
        last = step == steps - 1
        nxt = []
        for p, ak in zip(ps, aks):
            ak2 = _split2(ak)
            lhs = p if last else jnp.concatenate([p, ak], axis=0)
            nxt.append(_dot_x3(_split2(lhs), ak2))
        ps = [p + n[:c] for p, n in zip(ps, nxt)]
        aks = [None if last else n[c:] for n in nxt]
    return ps


def _gdn_chunk_kernel(q_ref, k_ref, v_ref, z_ref, bcol_ref, gcol_ref, grow_ref, gain_ref, o_ref, s_out_ref, s_ref, *, c, hb):
    ch = pl.program_id(2)
    dk = GDN_HEAD_DIM

    @pl.when(ch == 0)
    def _():
        s_ref[...] = jnp.zeros_like(s_ref)

    incl, strict, diag = _tri_masks(c)
    eye = jnp.where(diag, 1.0, 0.0).astype(F32)
    tri_b = jnp.where(incl, 1.0, 0.0).astype(BF16)
    gc_col = _cumsum_rows(tri_b, gcol_ref[...])
    g3 = _split3(grow_ref[...])
    gc_row = _dot_nt(g3[0], tri_b) + (_dot_nt(g3[1], tri_b) + _dot_nt(g3[2], tri_b))
    heads = range(hb)
    vs = [slice(hh * dk, (hh + 1) * dk) for hh in heads]
    q = [q_ref[:, (hh // 2) * dk:(hh // 2 + 1) * dk] for hh in heads]
    k = [k_ref[:, (hh // 2) * dk:(hh // 2 + 1) * dk] for hh in heads]
    beta = [bcol_ref[:, hh:hh + 1] for hh in heads]
    gcc = [gc_col[:, hh:hh + 1] for hh in heads]
    g_last = [g[c - 1:c, :] for g in gcc]
    decay = [jnp.where(incl, jnp.exp(jnp.where(incl, gcc[hh] - gc_row[hh:hh + 1, :], 0.0)), 0.0) for hh in heads]
    kb = [k[hh] * beta[hh] for hh in heads]
    kbf = [_bf(x) for x in k]
    kk = [_dot_nt(_bf(kb[hh]), kbf[hh]) for hh in heads]
    qk = [_dot_nt(_bf(q[hh]), kbf[hh]) for hh in heads]
    neg_lower = [jnp.where(strict, -(kk[hh] * decay[hh]), 0.0) for hh in heads]
    attn = [_bf(jnp.where(incl, qk[hh] * decay[hh], 0.0)) for hh in heads]
    e_gc = [jnp.exp(g) for g in gcc]
    rhs = [_split2(jnp.concatenate([v_ref[:, vs[hh]] * beta[hh], kb[hh] * e_gc[hh]], axis=-1)) for hh in heads]
    q_dec = [_bf(q[hh] * e_gc[hh]) for hh in heads]
    k_dec = [_bf(k[hh] * jnp.exp(g_last[hh] - gcc[hh])) for hh in heads]
    t_inv = _neumann_inverses(neg_lower, eye, c)
    sol = [_dot_x3(_split2(t_inv[hh]), rhs[hh]) for hh in heads]
    s = [s_ref[hh] for hh in heads]
    sb = [_bf(x) for x in s]
    vnb = [_bf(sol[hh][:, :dk] - _dot(_bf(sol[hh][:, dk:]), sb[hh])) for hh in heads]
    o = [_dot(q_dec[hh], sb[hh]) + _dot(attn[hh], vnb[hh]) for hh in heads]
    s_new = [s[hh] * jnp.exp(g_last[hh]) + _dot_tn(k_dec[hh], vnb[hh]) for hh in heads]
    for hh in heads:
        s_ref[hh] = s_new[hh]
        z = z_ref[:, vs[hh]]
        o_ref[:, vs[hh]] = _bf(_rms(o[hh], gain_ref[...]) * (z * _sigmoid(z)))

    @pl.when(ch == pl.num_programs(2) - 1)
    def _():
        s_out_ref[...] = s_ref[...]


def gdn_chunk_scan(q, k, v, z, beta, g, gain, *, batch, seq):
    m = q.shape[0]
    n_heads = v.shape[1] // GDN_HEAD_DIM
    c = min(CHUNK, seq)
    hb = min(16, n_heads)
    n_groups = n_heads // hb
    n = seq // c
    bcol = beta.reshape(m, n_groups, hb).transpose(1, 0, 2)
    gcol = g.reshape(m, n_groups, hb).transpose(1, 0, 2)
    grow = g.reshape(batch * n, c, n_groups, hb).transpose(0, 2, 3, 1)
    qk_w = (hb // 2) * GDN_HEAD_DIM
    v_w = hb * GDN_HEAD_DIM
    row_blk = lambda b, h, t: (b * n + t, h)
    col_spec = pl.BlockSpec((None, c, hb), lambda b, h, t: (h, b * n + t, 0))
    return pl.pallas_call(
        functools.partial(_gdn_chunk_kernel, c=c, hb=hb),
        grid=(batch, n_groups, n),
        in_specs=[
            pl.BlockSpec((c, qk_w), row_blk),
            pl.BlockSpec((c, qk_w), row_blk),
            pl.BlockSpec((c, v_w), row_blk),
            pl.BlockSpec((c, v_w), row_blk),
            col_spec, col_spec,
            pl.BlockSpec((None, None, hb, c), lambda b, h, t: (b * n + t, h, 0, 0)),
            pl.BlockSpec((1, GDN_HEAD_DIM), lambda b, h, t: (0, 0)),
        ],
        out_specs=[
            pl.BlockSpec((c, v_w), row_blk),
            pl.BlockSpec((None, hb, GDN_HEAD_DIM, GDN_HEAD_DIM), lambda b, h, t: (b, h, 0, 0)),
        ],
        out_shape=[jax.ShapeDtypeStruct((m, n_heads * GDN_HEAD_DIM), BF16),
                   jax.ShapeDtypeStruct((batch, n_heads, GDN_HEAD_DIM, GDN_HEAD_DIM), F32)],
        scratch_shapes=[pltpu.VMEM((hb, GDN_HEAD_DIM, GDN_HEAD_DIM), F32)],
        compiler_params=_params("parallel", "parallel", "arbitrary"),
        name="gdn_chunk_scan",
    )(q, k, v, z, bcol, gcol, grow, gain.reshape(1, GDN_HEAD_DIM))


def _gdn_step_kernel(q_ref, k_ref, v_ref, z_ref, beta_ref, g_ref, gain_ref, s0_ref, o_ref, s_ref):
    dk = GDN_HEAD_DIM
    n_heads = s0_ref.shape[0]
    n_qk = q_ref.shape[0]
    rep = n_heads // n_qk
    q_rows = q_ref[...]
    k_rows = k_ref[...]
    eye = (lax.broadcasted_iota(jnp.int32, (dk, dk), 0) == lax.broadcasted_iota(jnp.int32, (dk, dk), 1)).astype(F32)
    cols = _dot_nt_hi(eye, jnp.concatenate([q_rows, k_rows], axis=0))
    qk_dot = jnp.sum(q_rows * k_rows, axis=-1, keepdims=True)
    for j in range(n_qk):
        q_c = jnp.broadcast_to(cols[:, j:j + 1], (dk, dk))
        k_c = jnp.broadcast_to(cols[:, n_qk + j:n_qk + j + 1], (dk, dk))
        for h in range(j * rep, (j + 1) * rep):
            vs = slice(h * dk, (h + 1) * dk)
            beta = beta_ref[:, h:h + 1]
            e_g = jnp.exp(g_ref[:, h:h + 1])
            s0 = s0_ref[h]
            w_s = jnp.sum(s0 * k_c, axis=0, keepdims=True) * (beta * e_g)
            q_s = jnp.sum(s0 * q_c, axis=0, keepdims=True) * e_g
            v_new = v_ref[:, vs] * beta - w_s
            o = q_s + qk_dot[j:j + 1, :] * v_new
            s_ref[h] = s0 * e_g + k_c * v_new
            z = z_ref[:, vs]
            o_ref[:, vs] = _rms(o, gain_ref[...]) * (z * _sigmoid(z))


def gdn_step(q, k, v, z, beta, g, gain, s0):
    db, n_qk, dk = q.shape
    n_heads = s0.shape[1]
    vec = lambda w: pl.BlockSpec((None, 1, w), lambda b: (b, 0, 0))
    rows = pl.BlockSpec((None, n_qk, dk), lambda b: (b, 0, 0))
    st = pl.BlockSpec((None, n_heads, dk, dk), lambda b: (b, 0, 0, 0))
    return pl.pallas_call(
        _gdn_step_kernel,
        grid=(db,),
        in_specs=[rows, rows, vec(v.shape[2]), vec(v.shape[2]), vec(n_heads), vec(n_heads),
                  pl.BlockSpec((1, dk), lambda b: (0, 0)), st],
        out_specs=[vec(v.shape[2]), st],
        out_shape=[jax.ShapeDtypeStruct(v.shape, F32), jax.ShapeDtypeStruct(s0.shape, F32)],
        compiler_params=_params("parallel"),
        name="gdn_step",
    )(q, k, v, z, beta, g, gain.reshape(1, dk), s0)


def _rwkv_post(y, r, k, v, gate, rk, lnw, lnb):
    mean = jnp.mean(y, axis=-1, keepdims=True)
    var = jnp.mean(jnp.square(y - mean), axis=-1, keepdims=True)
    y = (y - mean) * lax.rsqrt(var + RWKV_GN_EPS) * lnw + lnb
    y = y + jnp.sum(r * k * rk, axis=-1, keepdims=True) * v
    return y * gate


def _rwkv_keys(k, a, kk_w, ka_w):
    kk = k * kk_w
    kk = kk * lax.rsqrt(jnp.sum(kk * kk, axis=-1, keepdims=True) + 1e-6)
    k = k * (1.0 + (a - 1.0) * ka_w)
    return k, -kk, kk * a


def _rwkv_chunk_kernel(r_ref, lw_ref, k_ref, v_ref, a_ref, gate_ref, kk_ref, ka_ref, rk_ref, lnw_ref, lnb_ref,
                       o_ref, s_out_ref, s_ref, *, c, hb):
    ch = pl.program_id(2)
    n = RWKV_HEAD_DIM

    @pl.when(ch == 0)
    def _():
        s_ref[...] = jnp.zeros_like(s_ref)

    incl, strict, diag = _tri_masks(c)
    eye = jnp.where(diag, 1.0, 0.0).astype(F32)
    tri_b = jnp.where(incl, 1.0, 0.0).astype(BF16)
    gi_all = _cumsum_rows(tri_b, lw_ref[...])
    heads = range(hb)
    sl = [slice(hh * n, (hh + 1) * n) for hh in heads]
    r = [r_ref[:, s_] for s_ in sl]
    v = [v_ref[:, s_] for s_ in sl]
    vb = [_bf(x) for x in v]
    keys = [_rwkv_keys(k_ref[:, s_], a_ref[:, s_], kk_ref[:, s_], ka_ref[:, s_]) for s_ in sl]
    k = [t[0] for t in keys]
    a_vec = [t[1] for t in keys]
    b_vec = [t[2] for t in keys]
    gi = [gi_all[:, s_] for s_ in sl]
    g_last = [g[c - 1:c, :] for g in gi]
    e_neg = [jnp.exp(-g) for g in gi]
    e_rest = [jnp.exp(g_last[hh] - gi[hh]) for hh in heads]
    a_t = [a_vec[hh] * jnp.exp(gi[hh] - lw_ref[:, sl[hh]]) for hh in heads]
    r_tb = [_bf(r[hh] * jnp.exp(gi[hh])) for hh in heads]
    lhs = [jnp.concatenate([_bf(a_t[hh]), r_tb[hh]], axis=0) for hh in heads]
    rhs = [_bf(jnp.concatenate([b_vec[hh] * e_neg[hh], k[hh] * e_neg[hh]], axis=0)) for hh in heads]
    tail = [_bf(jnp.concatenate([b_vec[hh] * e_rest[hh], k[hh] * e_rest[hh]], axis=0)) for hh in heads]
    m4 = [_dot_nt(lhs[hh], rhs[hh]) for hh in heads]
    a_ab = [jnp.where(strict, m[:c, :c], 0.0) for m in m4]
    a_ak = [_bf(jnp.where(strict, m[:c, c:], 0.0)) for m in m4]
    a_rb = [_bf(jnp.where(incl, m[c:, :c], 0.0)) for m in m4]
    a_rk = [_bf(jnp.where(incl, m[c:, c:], 0.0)) for m in m4]
    akv = [_dot(a_ak[hh], vb[hh]) for hh in heads]
    t_inv = _neumann_inverses(a_ab, eye, c)
    sol = [_dot_x3(_split2(t_inv[hh]), _split2(jnp.concatenate([a_t[hh], akv[hh]], axis=-1))) for hh in heads]
    s = [s_ref[hh] for hh in heads]
    sb = [_bf(x) for x in s]
    ub = [_bf(_dot_nt(_bf(sol[hh][:, :n]), sb[hh]) + sol[hh][:, n:]) for hh in heads]
    y = [_dot_nt(r_tb[hh], sb[hh]) + _dot(a_rb[hh], ub[hh]) + _dot(a_rk[hh], vb[hh]) for hh in heads]
    s_new = [s[hh] * jnp.exp(g_last[hh]) + _dot_tn(jnp.concatenate([ub[hh], vb[hh]], axis=0), tail[hh])
             for hh in heads]
    for hh in heads:
        s_ref[hh] = s_new[hh]
        o_ref[:, sl[hh]] = _bf(_rwkv_post(y[hh], r[hh], k[hh], v[hh], gate_ref[:, sl[hh]], rk_ref[:, sl[hh]],
                                          lnw_ref[:, sl[hh]], lnb_ref[:, sl[hh]]))

    @pl.when(ch == pl.num_programs(2) - 1)
    def _():
        s_out_ref[...] = s_ref[...]


def rwkv_chunk_scan(r, lw, k, v, a, gate, kk_w, ka_w, rk, lnw, lnb, *, batch, seq):
    m, d = r.shape
    n_heads = d // RWKV_HEAD_DIM
    hb = min(16, n_heads)
    w = hb * RWKV_HEAD_DIM
    c = min(CHUNK, seq)
    n = seq // c
    tok = pl.BlockSpec((c, w), lambda b, h, t: (b * n + t, h))
    par = pl.BlockSpec((1, w), lambda b, h, t: (0, h))
    return pl.pallas_call(
        functools.partial(_rwkv_chunk_kernel, c=c, hb=hb),
        grid=(batch, n_heads // hb, n),
        in_specs=[tok] * 6 + [par] * 5,
        out_specs=[tok, pl.BlockSpec((None, hb, RWKV_HEAD_DIM, RWKV_HEAD_DIM), lambda b, h, t: (b, h, 0, 0))],
        out_shape=[jax.ShapeDtypeStruct((m, d), BF16),
                   jax.ShapeDtypeStruct((batch, n_heads, RWKV_HEAD_DIM, RWKV_HEAD_DIM), F32)],
        scratch_shapes=[pltpu.VMEM((hb, RWKV_HEAD_DIM, RWKV_HEAD_DIM), F32)],
        compiler_params=_params("parallel", "parallel", "arbitrary"),
        name="rwkv_chunk_scan",
    )(r, lw, k, v, a, gate, kk_w, ka_w, rk, lnw, lnb)


def _rwkv_step_kernel(r_ref, lw_ref, k_ref, v_ref, a_ref, gate_ref, kk_ref, ka_ref, rk_ref, lnw_ref, lnb_ref,
                      s0_ref, o_ref, s_ref, y_ref):
    n = s0_ref.shape[0]
    r = r_ref[...]
    v = v_ref[...]
    k0 = k_ref[...]
    a = a_ref[...]
    kk = k0 * kk_ref[...]
    kk = kk * lax.rsqrt(jnp.sum(kk * kk, axis=0, keepdims=True) + 1e-6)
    k = k0 * (1.0 + (a - 1.0) * ka_ref[...])
    a_vec = -kk
    b_vec = kk * a
    w = jnp.exp(lw_ref[...])

    def body(i, carry):
        s0 = s0_ref[i]
        sa = jnp.sum(s0 * a_vec, axis=0, keepdims=True)
        s = s0 * w + sa * b_vec + v_ref[pl.ds(i, 1), :] * k
        s_ref[i] = s
        y_ref[pl.ds(i, 1), :] = jnp.sum(s * r, axis=0, keepdims=True)
        return carry

    lax.fori_loop(0, n, body, 0, unroll=8)
    y = y_ref[...]
    mean = jnp.mean(y, axis=0, keepdims=True)
    var = jnp.mean(jnp.square(y - mean), axis=0, keepdims=True)
    y = (y - mean) * lax.rsqrt(var + RWKV_GN_EPS) * lnw_ref[...] + lnb_ref[...]
    y = y + jnp.sum(r * k * rk_ref[...], axis=0, keepdims=True) * v
    o_ref[...] = y * gate_ref[...]


def rwkv_step(r, lw, k, v, a, gate, kk_w, ka_w, rk, lnw, lnb, s0_t):
    d, db = r.shape
    n = RWKV_HEAD_DIM
    n_heads = d // n
    tok = pl.BlockSpec((n, db), lambda h: (h, 0))
    par = pl.BlockSpec((n, 1), lambda h: (h, 0))
    st = pl.BlockSpec((None, n, n, db), lambda h: (h, 0, 0, 0))
    return pl.pallas_call(
        _rwkv_step_kernel,
        grid=(n_heads,),
        in_specs=[tok] * 6 + [par] * 5 + [st],
        out_specs=[tok, st],
        out_shape=[jax.ShapeDtypeStruct((d, db), F32), jax.ShapeDtypeStruct(s0_t.shape, F32)],
        scratch_shapes=[pltpu.VMEM((n, db), F32)],
        compiler_params=_params("parallel"),
        name="rwkv_step",
    )(r, lw, k, v, a, gate, kk_w, ka_w, rk, lnw, lnb, s0_t)


def _head_rms(x, gain, n_heads):
    m = x.shape[0]
    xh = x.reshape(m, n_heads, -1)
    return (xh * lax.rsqrt(jnp.mean(xh * xh, axis=-1, keepdims=True) + NORM_EPS) * gain).reshape(m, -1)


def _fox_project(x, gain, w_in, b_f, q_gain, k_gain):
    width = (w_in.shape[1] // LANES) * LANES
    n_heads = w_in.shape[1] - width
    w = _bf(jnp.concatenate([w_in[:, :width], _pad_cols(w_in[:, width:], LANES)], axis=1))
    h = mm(x, w, gain=gain)
    d = width // 4
    q = _head_rms(h[:, :d], q_gain, n_heads) * (FOX_HEAD_DIM ** -0.5)
    k = _head_rms(h[:, d:2 * d], k_gain, n_heads)
    v = h[:, 2 * d:3 * d]
    gate = _sigmoid(h[:, 3 * d:4 * d])
    log_f = jax.nn.log_sigmoid(h[:, width:width + n_heads] + b_f)
    return q, k, v, gate, log_f


def _fox_prompt(x, gain, w_in, b_f, q_gain, k_gain, batch, seq):
    m = batch * seq
    dh = FOX_HEAD_DIM
    width = (w_in.shape[1] // LANES) * LANES
    n_heads = w_in.shape[1] - width
    d = width // 4
    qv_t = mm_t(x, _bf(jnp.concatenate([w_in[:, :d], w_in[:, 2 * d:3 * d]], axis=1).T), gain, batch=batch, seq=seq)
    rest = jnp.concatenate([w_in[:, d:2 * d], w_in[:, 3 * d:4 * d], _pad_cols(w_in[:, width:], 2 * LANES)], axis=1)
    h = mm(x, _bf(rest), gain=gain)
    k = _head_rms(h[:, :d], k_gain, n_heads)
    lf = jax.nn.log_sigmoid(h[:, 2 * d:2 * d + n_heads] + b_f)
    cum = jnp.cumsum(lf.reshape(batch, seq, n_heads), axis=1).reshape(m, n_heads)
    q_t = qv_t[:, :d].reshape(batch, n_heads, dh, seq)
    q_t = q_t * lax.rsqrt(jnp.mean(q_t * q_t, axis=2, keepdims=True) + NORM_EPS) * (q_gain * dh ** -0.5)[:, None]
    v_t = qv_t[:, d:].reshape(batch, n_heads, dh, seq)
    rows = lambda n, val: jnp.full((batch, n_heads, n, seq), val, BF16)
    qt = jnp.concatenate([_bf(q_t), rows(3, 1.0), rows(LANES - dh - 3, 0.0)], axis=2)
    vt = jnp.concatenate([_bf(v_t), rows(1, 1.0), rows(LANES - dh - 1, 0.0)], axis=2)
    neg_c3 = jnp.stack(_split3(-cum), axis=-1)
    ka = jnp.concatenate([_bf(k).reshape(m, n_heads, dh), neg_c3, jnp.zeros((m, n_heads, LANES - dh - 3), BF16)],
                         axis=-1).reshape(m, n_heads * LANES)
    pair = lambda a: a.reshape(batch, n_heads // 2, 2, LANES, seq)
    return pair(qt), ka, pair(vt), h, d // LANES, k, v_t.transpose(0, 3, 1, 2), lf


def fox_layer(hp, hs, gain, w_in, b_f, q_gain, k_gain, w_out, layer_idx, cache_k_t, cache_v_t, cache_logf_t,
              page_table, batch, seq):
    n_heads = b_f.shape[0]
    dh = FOX_HEAD_DIM
    w_out_b = _bf(w_out)
    qt, ka, vt, gate_src, gate_col, k, v, lf = _fox_prompt(hp, gain, w_in, b_f, q_gain, k_gain, batch, seq)
    og = fox_flash(qt, ka, vt, gate_src, gate_col, batch=batch, seq=seq)
    hp = mm(og, w_out_b, res=hp)
    outs_p = (k.reshape(batch, seq, n_heads, dh), v, lf.reshape(batch, seq, n_heads))
    db = hs.shape[0]
    q, k, v, gate, lf = _fox_project(hs, gain, w_in, b_f, q_gain, k_gain)
    og = fox_decode(layer_idx, page_table, q.reshape(db, n_heads, dh), k.reshape(db, n_heads, dh),
                    v.reshape(db, n_heads, dh), lf.reshape(db, n_heads, 1), gate.reshape(db, 1, n_heads * dh),
                    cache_logf_t, cache_k_t, cache_v_t)
    hs = mm(og.reshape(db, n_heads * dh), w_out_b, res=hs)
    outs_s = (k.reshape(db, 1, n_heads, dh), v.reshape(db, 1, n_heads, dh), lf.reshape(db, 1, n_heads))
    return hp, hs, outs_p, outs_s


def _gdn_project(x, gain, w_in, n_heads):
    total = w_in.shape[1]
    main = total - 2 * n_heads
    w = _bf(jnp.concatenate([w_in[:, :main], _pad_cols(w_in[:, main:], LANES)], axis=1))
    h = mm(x, w, gain=gain)
    conv_ch = main - n_heads * GDN_HEAD_DIM
    return h[:, :conv_ch], h[:, conv_ch:main], h[:, main:main + n_heads], h[:, main + n_heads:main + 2 * n_heads]


def _gdn_post_conv(conv, a, beta_logit, a_log, dt_bias, n_heads):
    m = conv.shape[0]
    conv = conv * _sigmoid(conv)
    qk_w = (n_heads // 2) * GDN_HEAD_DIM

    def l2(x):
        xh = x.reshape(m, n_heads // 2, GDN_HEAD_DIM)
        return (xh * lax.rsqrt(jnp.sum(xh * xh, axis=-1, keepdims=True) + 1e-6)).reshape(m, qk_w)

    q = l2(conv[:, :qk_w]) * (GDN_HEAD_DIM ** -0.5)
    k = l2(conv[:, qk_w:2 * qk_w])
    v = conv[:, 2 * qk_w:]
    beta = _sigmoid(beta_logit)
    g = -jnp.exp(a_log) * jax.nn.softplus(a + dt_bias)
    return q, k, v, beta, g


def gdn_layer(hp, hs, gain, w_in, conv_w, a_log, dt_bias, out_gain, w_out, conv_state, s_state, batch, seq):
    n_heads = a_log.shape[0]
    w_out_b = _bf(w_out)
    qkv, z, a, bl = _gdn_project(hp, gain, w_in, n_heads)
    ch = qkv.shape[1]
    xc = jnp.concatenate([jnp.zeros((batch, GDN_CONV_W - 1, ch), F32), qkv.reshape(batch, seq, ch)], axis=1)
    conv = xc[:, 0:seq] * conv_w[0]
    for i in range(1, GDN_CONV_W):
        conv = conv + xc[:, i:i + seq] * conv_w[i]
    q, k, v, beta, g = _gdn_post_conv(conv.reshape(batch * seq, ch), a, bl, a_log, dt_bias, n_heads)
    og, s_p = gdn_chunk_scan(q, k, v, z, beta, g, out_gain, batch=batch, seq=seq)
    hp = mm(og, w_out_b, res=hp)
    conv_p = xc[:, seq:]
    db = hs.shape[0]
    qkv, z, a, bl = _gdn_project(hs, gain, w_in, n_heads)
    xc = jnp.concatenate([conv_state, qkv[:, None, :]], axis=1)
    conv = xc[:, 0] * conv_w[0]
    for i in range(1, GDN_CONV_W):
        conv = conv + xc[:, i] * conv_w[i]
    q, k, v, beta, g = _gdn_post_conv(conv, a, bl, a_log, dt_bias, n_heads)
    og, s_s = gdn_step(q.reshape(db, -1, GDN_HEAD_DIM), k.reshape(db, -1, GDN_HEAD_DIM), v[:, None], z[:, None],
                       beta[:, None], g[:, None], out_gain, s_state)
    hs = mm(og.reshape(db, -1), w_out_b, res=hs)
    conv_s = xc[:, 1:]
    return hp, hs, (s_p, conv_p), (s_s, conv_s)


def _rwkv_project(u, u_prev, mu, w0, w1, w2, a0, a1, a2, g1, g2, w_r, w_k, w_v):
    du = u_prev - u
    xr, xw, xk, xv, xa, xg = (u + du * mu[i] for i in range(6))
    lora = lambda w: -(-w.shape[1] // LANES) * LANES
    r = mm(xr, _bf(w_r))
    k = mm(xk, _bf(w_k))
    v = mm(xv, _bf(w_v))
    nw, na, ng = lora(w1), lora(a1), lora(g1)
    w_l = mm(jnp.tanh(mm(xw, _bf(_pad_cols(w1, nw)))), _bf(_pad_rows(w2, nw)))
    a_l = mm(mm(xa, _bf(_pad_cols(a1, na))), _bf(_pad_rows(a2, na)))
    gate = mm(_sigmoid(mm(xg, _bf(_pad_cols(g1, ng)))), _bf(_pad_rows(g2, ng)))
    w_raw = -jax.nn.softplus(-(w0 + w_l)) - 0.5
    log_decay = -jnp.exp(w_raw)
    a = _sigmoid(a0 + a_l)
    return r, log_decay, k, v, a, gate


def rwkv_layer(hp, hs, gain, mu, w0, w1, w2, a0, a1, a2, g1, g2, k_k, k_a, r_k, w_r, w_k, w_v, w_o, ln_w, ln_b,
               shift_state, wkv_state, batch, seq):
    d = hp.shape[1]
    w_o_b = _bf(w_o)
    proj_w = (mu, w0, w1, w2, a0, a1, a2, g1, g2, w_r, w_k, w_v)
    chan = (k_k, k_a, r_k.reshape(d), ln_w, ln_b)
    u = _rms(hp, gain).reshape(batch, seq, d)
    u_prev = jnp.concatenate([jnp.zeros((batch, 1, d), F32), u[:, :-1]], axis=1)
    toks = _rwkv_project(u.reshape(batch * seq, d), u_prev.reshape(batch * seq, d), *proj_w)
    og, s_p = rwkv_chunk_scan(*toks, *(p.reshape(1, d) for p in chan), batch=batch, seq=seq)
    hp = mm(og, w_o_b, res=hp)
    shift_p = u[:, -1]
    us = _rms(hs, gain)
    toks = _rwkv_project(us, shift_state, *proj_w)
    og_t, s_t = rwkv_step(*(t.T for t in toks), *(p.reshape(d, 1) for p in chan), wkv_state.transpose(1, 2, 3, 0))
    hs = mm(og_t.T, w_o_b, res=hs)
    return hp, hs, (shift_p, s_p), (us, s_t.transpose(3, 0, 1, 2))


def kernel(x_prompt, x_sample, cache_k, cache_v, cache_logf, page_table, state_gdn_s, state_gdn_conv, state_rwkv_shift, state_rwkv_wkv, norm_mix, norm_ffn, norm_final, fox_w_in, fox_b_f, fox_q_norm, fox_k_norm, fox_w_out, gdn_w_in, gdn_conv_w, gdn_a_log, gdn_dt_bias, gdn_out_norm, gdn_w_out, rwkv_mu, rwkv_w0, rwkv_w1, rwkv_w2, rwkv_a0, rwkv_a1, rwkv_a2, rwkv_g1, rwkv_g2, rwkv_k_k, rwkv_k_a, rwkv_r_k, rwkv_w_r, rwkv_w_k, rwkv_w_v, rwkv_w_o, rwkv_ln_w, rwkv_ln_b, ffn_w_in, ffn_w_out):
    batch, seq, d = x_prompt.shape
    db = x_sample.shape[0]
    depth = norm_mix.shape[0]
    hp = x_prompt.reshape(batch * seq, d)
    hs = x_sample.reshape(db, d)
    cache_k_t = cache_k.transpose(0, 1, 3, 4, 2)
    cache_v_t = cache_v.transpose(0, 1, 3, 4, 2)
    cache_logf_t = cache_logf.transpose(0, 1, 3, 2)
    fox_p, fox_s, gdn_p, gdn_s, rwkv_p, rwkv_s = [], [], [], [], [], []
    for layer in range(depth):
        kind, j = layer % 3, layer // 3
        if kind == 0:
            hp, hs, op, os_ = fox_layer(hp, hs, norm_mix[layer], fox_w_in[j], fox_b_f[j], fox_q_norm[j], fox_k_norm[j],
                                        fox_w_out[j], j, cache_k_t, cache_v_t, cache_logf_t, page_table, batch, seq)
            fox_p.append(op)
            fox_s.append(os_)
        elif kind == 1:
            hp, hs, op, os_ = gdn_layer(hp, hs, norm_mix[layer], gdn_w_in[j], gdn_conv_w[j], gdn_a_log[j],
                                        gdn_dt_bias[j], gdn_out_norm[j], gdn_w_out[j], state_gdn_conv[j],
                                        state_gdn_s[j], batch, seq)
            gdn_p.append(op)
            gdn_s.append(os_)
        else:
            hp, hs, op, os_ = rwkv_layer(hp, hs, norm_mix[layer], rwkv_mu[j], rwkv_w0[j], rwkv_w1[j], rwkv_w2[j],
                                         rwkv_a0[j], rwkv_a1[j], rwkv_a2[j], rwkv_g1[j], rwkv_g2[j], rwkv_k_k[j],
                                         rwkv_k_a[j], rwkv_r_k[j], rwkv_w_r[j], rwkv_w_k[j], rwkv_w_v[j],
                                         rwkv_w_o[j], rwkv_ln_w[j], rwkv_ln_b[j], state_rwkv_shift[j],
                                         state_rwkv_wkv[j], batch, seq)
            rwkv_p.append(op)
            rwkv_s.append(os_)
        final = norm_final if layer == depth - 1 else None
        w_in_b, w_out_b = _bf(ffn_w_in[layer]), _bf(ffn_w_out[layer])
        hp = ffn(hp, norm_ffn[layer], w_in_b, w_out_b, final)
        hs = ffn(hs, norm_ffn[layer], w_in_b, w_out_b, final)
    stack = lambda items, i: jnp.stack([it[i] for it in items])
    return (hp.reshape(batch, seq, d), hs.reshape(db, 1, d),
            stack(fox_p, 0), stack(fox_p, 1), stack(fox_p, 2),
            stack(fox_s, 0), stack(fox_s, 1), stack(fox_s, 2),
            stack(gdn_p, 0), stack(gdn_p, 1), stack(gdn_s, 0), stack(gdn_s, 1),
            stack(rwkv_p, 0), stack(rwkv_p, 1), stack(rwkv_s, 0), stack(rwkv_s, 1))
```
